```python
import jax, jax.numpy as jnp
from jax import lax
import numpy as np

D_MODEL = 1024
BATCH = 16
SEQ = 2048
DEPTH = 1

GRID_W = 64
CTX_LEN = 256
HEAD_DIM = 128
N_Q_HEADS = 8
N_KV_HEADS = 2
Q_PER_KV = N_Q_HEADS // N_KV_HEADS
ROPE_AXIS_DIM = HEAD_DIM // 2
ROPE_THETA = 10000.0
Q_BLOCK = 128
GLA_HEADS = 4
GLA_DK = (D_MODEL // 2) // GLA_HEADS
GLA_DV = D_MODEL // GLA_HEADS
GLA_LOWRANK = 16
GLA_GATE_NORM = 16.0
GLA_CHUNK = 64
N_BRANCH = 2
D_FF = ((8 * D_MODEL + 3 * 256 - 1) // (3 * 256)) * 256
EPS = 1e-6

ATTN_Q_W = N_Q_HEADS * HEAD_DIM
ATTN_KV_W = N_KV_HEADS * HEAD_DIM
GLA_QK_W = GLA_HEADS * GLA_DK
GLA_V_W = GLA_HEADS * GLA_DV
IN_WIDTHS = (ATTN_Q_W, ATTN_KV_W, ATTN_KV_W, GLA_QK_W, GLA_QK_W, GLA_V_W, GLA_V_W, 2 * GLA_LOWRANK, N_BRANCH * D_MODEL)
D_IN = sum(IN_WIDTHS)

kernel_name = "hybrid_gqa_gla_adaln_prefix_block"


def rms_norm(x, g):
    x32 = x.astype(jnp.float32)
    y = x32 * lax.rsqrt(jnp.mean(x32 * x32, axis=-1, keepdims=True) + EPS)
    return y.astype(x.dtype) * g


def modulate(h, shift, scale):
    return h * (1 + scale) + shift


def adaln(cvec, w_ada, b_ada):
    mod = jax.nn.silu(cvec) @ w_ada + b_ada
    return jnp.split(mod, 6, axis=-1)


def split_in(p):
    idx = [int(i) for i in np.cumsum(IN_WIDTHS)[:-1]]
    return jnp.split(p, idx, axis=-1)


def rope_2d_tables(rows, dtype):
    row = jnp.repeat(jnp.arange(rows, dtype=jnp.float32), GRID_W)
    col = jnp.tile(jnp.arange(GRID_W, dtype=jnp.float32), rows)
    inv_freq = 1.0 / (ROPE_THETA ** (jnp.arange(0, ROPE_AXIS_DIM, 2, dtype=jnp.float32) / ROPE_AXIS_DIM))
    ang = jnp.concatenate([row[:, None] * inv_freq[None], col[:, None] * inv_freq[None]], axis=-1)
    return jnp.cos(ang).astype(dtype), jnp.sin(ang).astype(dtype)


def apply_rope(x, cos, sin):
    xr = x.reshape(x.shape[:-1] + (HEAD_DIM // 2, 2))
    x0, x1 = xr[..., 0], xr[..., 1]
    c, s = cos[None, :, None, :], sin[None, :, None, :]
    out = jnp.stack([x0 * c - x1 * s, x0 * s + x1 * c], axis=-1)
    return out.reshape(x.shape)


def grouped_attention(qg, k, v):
    s = jnp.einsum('bqhgd,bkhd->bhgqk', qg, k).astype(jnp.float32) * (HEAD_DIM ** -0.5)
    p = jax.nn.softmax(s, axis=-1).astype(v.dtype)
    return jnp.einsum('bhgqk,bkhd->bqhgd', p, v)


def latent_attention(q, k_all, v_all):
    B, T = q.shape[0], q.shape[1]
    nb = T // Q_BLOCK
    qb = jnp.moveaxis(q.reshape(B, nb, Q_BLOCK, N_KV_HEADS, Q_PER_KV, HEAD_DIM), 1, 0)
    o = lax.map(lambda qi: grouped_attention(qi, k_all, v_all), qb)
    return jnp.moveaxis(o, 0, 1).reshape(B, T, ATTN_Q_W)


def gla_chunked(q, k, v, log_a, s0):
    B, T, H, dk = q.shape
    dv = v.shape[-1]
    C = GLA_CHUNK
    N = T // C
    f32 = jnp.float32
    qc = q.astype(f32).reshape(B, N, C, H, dk) * (dk ** -0.5)
    kc = k.astype(f32).reshape(B, N, C, H, dk)
    vc = v.astype(f32).reshape(B, N, C, H, dv)
    b = jnp.cumsum(log_a.astype(f32).reshape(B, N, C, H, dk), axis=2)
    b_last = b[:, :, -1]
    qe = qc * jnp.exp(b)
    ke = kc * jnp.exp(-b)
    kd = kc * jnp.exp(b_last[:, :, None] - b)
    mask = jnp.tril(jnp.ones((C, C), dtype=bool))
    a_intra = jnp.where(mask, jnp.einsum('bnthk,bnshk->bnhts', qe, ke), 0.0)
    o_intra = jnp.einsum('bnhts,bnshv->bnthv', a_intra, vc)

    def step(state, inp):
        qe_c, kd_c, v_c, dl_c = inp
        o = jnp.einsum('bchk,bhkv->bchv', qe_c, state)
        state = state * dl_c[..., None] + jnp.einsum('bchk,bchv->bhkv', kd_c, v_c)
        return state, o

    xs = (jnp.moveaxis(qe, 1, 0), jnp.moveaxis(kd, 1, 0), jnp.moveaxis(vc, 1, 0), jnp.moveaxis(jnp.exp(b_last), 1, 0))
    s_fin, o_inter = lax.scan(step, s0.astype(f32), xs)
    o = o_intra + jnp.moveaxis(o_inter, 0, 1)
    return o.reshape(B, T, H, dv).astype(q.dtype), s_fin


def gla_bidir(q, k, v, la_f, la_b, s0_f, s0_b):
    o_f, s_f = gla_chunked(q, k, v, la_f, s0_f)
    flip = lambda t: jnp.flip(t, axis=1)
    o_b, s_b = gla_chunked(flip(q), flip(k), flip(v), flip(la_b), s0_b)
    return o_f + flip(o_b), s_f, s_b


def gla_decays(lowrank, up_f, up_f_b, up_b, up_b_b):
    B, T = lowrank.shape[0], lowrank.shape[1]
    lr_f, lr_b = jnp.split(lowrank, 2, axis=-1)
    la_f = jax.nn.log_sigmoid((lr_f @ up_f + up_f_b).astype(jnp.float32)) / GLA_GATE_NORM
    la_b = jax.nn.log_sigmoid((lr_b @ up_b + up_b_b).astype(jnp.float32)) / GLA_GATE_NORM
    return la_f.reshape(B, T, GLA_HEADS, GLA_DK), la_b.reshape(B, T, GLA_HEADS, GLA_DK)


def gla_heads(gq, gk, gv):
    B, T = gq.shape[0], gq.shape[1]
    return (gq.reshape(B, T, GLA_HEADS, GLA_DK), gk.reshape(B, T, GLA_HEADS, GLA_DK),
            gv.reshape(B, T, GLA_HEADS, GLA_DV))


def branch_merge(attn_o, gla_o, gla_g, merge_g, gla_norm_g, w_attn_proj, w_gla_proj, w_out):
    B, T = attn_o.shape[0], attn_o.shape[1]
    go = rms_norm(gla_o, gla_norm_g).reshape(B, T, GLA_V_W) * jax.nn.silu(gla_g)
    ya = attn_o @ w_attn_proj
    yg = go @ w_gla_proj
    ga, gb = jnp.split(merge_g, 2, axis=-1)
    return (jax.nn.sigmoid(ga) * ya + jax.nn.sigmoid(gb) * yg) @ w_out


def swiglu(h, w_ffn_in, w_ffn_out):
    a, b = jnp.split(h @ w_ffn_in, 2, axis=-1)
    return (jax.nn.silu(a) * b) @ w_ffn_out


def setup_inputs(seed: int = 0) -> dict:
    key = jax.random.key(seed)
    ks = jax.random.split(key, 21)
    nrm = jax.random.normal
    f32 = jnp.float32

    def dense(k, shape, fan_in, gain=1.0):
        return nrm(k, shape, f32) * (gain * fan_in ** -0.5)

    return {
        "x": nrm(ks[0], (BATCH, SEQ, D_MODEL), f32),
        "c": nrm(ks[1], (BATCH, D_MODEL), f32),
        "ctx": nrm(ks[2], (BATCH, CTX_LEN, D_MODEL), f32),
        "c_ctx": nrm(ks[3], (D_MODEL,), f32),
        "w_ada": dense(ks[4], (DEPTH, D_MODEL, 6 * D_MODEL), D_MODEL, 0.5),
        "b_ada": 0.02 * nrm(ks[5], (DEPTH, 6 * D_MODEL), f32),
        "norm1_g": 1.0 + 0.05 * nrm(ks[6], (DEPTH, D_MODEL), f32),
        "w_in": dense(ks[7], (DEPTH, D_MODEL, D_IN), D_MODEL),
        "q_norm_g": 1.0 + 0.05 * nrm(ks[8], (DEPTH, HEAD_DIM), f32),
        "k_norm_g": 1.0 + 0.05 * nrm(ks[9], (DEPTH, HEAD_DIM), f32),
        "gk_up_f": dense(ks[10], (DEPTH, GLA_LOWRANK, GLA_QK_W), GLA_LOWRANK),
        "gk_up_f_b": 0.1 * nrm(ks[11], (DEPTH, GLA_QK_W), f32),
        "gk_up_b": dense(ks[12], (DEPTH, GLA_LOWRANK, GLA_QK_W), GLA_LOWRANK),
        "gk_up_b_b": 0.1 * nrm(ks[13], (DEPTH, GLA_QK_W), f32),
        "gla_norm_g": 1.0 + 0.05 * nrm(ks[14], (DEPTH, GLA_DV), f32),
        "w_attn_proj": dense(ks[15], (DEPTH, ATTN_Q_W, D_MODEL), ATTN_Q_W),
        "w_gla_proj": dense(ks[16], (DEPTH, GLA_V_W, D_MODEL), GLA_V_W),
        "w_out": dense(ks[17], (DEPTH, D_MODEL, D_MODEL), D_MODEL),
        "norm2_g": 1.0 + 0.05 * nrm(ks[18], (DEPTH, D_MODEL), f32),
        "w_ffn_in": dense(ks[19], (DEPTH, D_MODEL, 2 * D_FF), D_MODEL),
        "w_ffn_out": dense(ks[20], (DEPTH, D_FF, D_MODEL), D_FF),
    }


def reference(x, c, ctx, c_ctx, w_ada, b_ada, norm1_g, w_in, q_norm_g, k_norm_g, gk_up_f, gk_up_f_b,
              gk_up_b, gk_up_b_b, gla_norm_g, w_attn_proj, w_gla_proj, w_out, norm2_g, w_ffn_in, w_ffn_out):
    B, T, _ = x.shape
    Tc = ctx.shape[1]
    ROWS = T // GRID_W
    cos, sin = rope_2d_tables(ROWS, x.dtype)

    for l in range(DEPTH):
        sh1, sc1, g1, sh2, sc2, g2 = [m[:, None, :] for m in adaln(c, w_ada[l], b_ada[l])]
        sh1c, sc1c, g1c, sh2c, sc2c, g2c = adaln(c_ctx, w_ada[l], b_ada[l])

        hc = modulate(rms_norm(ctx, norm1_g[l]), sh1c, sc1c)
        aq_c, ak_c, av_c, gq_c, gk_c, gv_c, gg_c, glr_c, mg_c = split_in(hc @ w_in[l])
        k_c = rms_norm(ak_c.reshape(B, Tc, N_KV_HEADS, HEAD_DIM), k_norm_g[l])
        v_c = av_c.reshape(B, Tc, N_KV_HEADS, HEAD_DIM)
        la_f_c, la_b_c = gla_decays(glr_c, gk_up_f[l], gk_up_f_b[l], gk_up_b[l], gk_up_b_b[l])
        s_zero = jnp.zeros((B, GLA_HEADS, GLA_DK, GLA_DV), jnp.float32)
        gla_c, s_f, s_b = gla_bidir(*gla_heads(gq_c, gk_c, gv_c), la_f_c, la_b_c, s_zero, s_zero)

        hx = modulate(rms_norm(x, norm1_g[l]), sh1, sc1)
        aq, ak, av, gq, gk, gv, gg, glr, mg = split_in(hx @ w_in[l])
        q_x = apply_rope(rms_norm(aq.reshape(B, T, N_Q_HEADS, HEAD_DIM), q_norm_g[l]), cos, sin)
        k_x = apply_rope(rms_norm(ak.reshape(B, T, N_KV_HEADS, HEAD_DIM), k_norm_g[l]), cos, sin)
        v_x = av.reshape(B, T, N_KV_HEADS, HEAD_DIM)
        attn_x = latent_attention(q_x, jnp.concatenate([k_c, k_x], axis=1), jnp.concatenate([v_c, v_x], axis=1))
        la_f, la_b = gla_decays(glr, gk_up_f[l], gk_up_f_b[l], gk_up_b[l], gk_up_b_b[l])
        gla_x, _, _ = gla_bidir(*gla_heads(gq, gk, gv), la_f, la_b, s_f, s_b)
        x = x + g1 * branch_merge(attn_x, gla_x, gg, mg, gla_norm_g[l], w_attn_proj[l], w_gla_proj[l], w_out[l])
        x = x + g2 * swiglu(modulate(rms_norm(x, norm2_g[l]), sh2, sc2), w_ffn_in[l], w_ffn_out[l])

        if l < DEPTH - 1:
            q_c = rms_norm(aq_c.reshape(B, Tc, N_Q_HEADS, HEAD_DIM), q_norm_g[l])
            attn_c = grouped_attention(q_c.reshape(B, Tc, N_KV_HEADS, Q_PER_KV, HEAD_DIM), k_c, v_c).reshape(B, Tc, ATTN_Q_W)
            ctx = ctx + g1c * branch_merge(attn_c, gla_c, gg_c, mg_c, gla_norm_g[l], w_attn_proj[l], w_gla_proj[l], w_out[l])
            ctx = ctx + g2c * swiglu(modulate(rms_norm(ctx, norm2_g[l]), sh2c, sc2c), w_ffn_in[l], w_ffn_out[l])

    return x
```

```python
import functools

import numpy as np
import jax
import jax.numpy as jnp
from jax import lax
from jax.experimental import pallas as pl
from jax.experimental.pallas import tpu as pltpu

F32 = jnp.float32
BF16 = jnp.bfloat16

D_MODEL = 1024
GRID_W = 64
HEAD_DIM = 128
N_Q_HEADS = 8
N_KV_HEADS = 2
Q_PER_KV = N_Q_HEADS // N_KV_HEADS
ROPE_THETA = 10000.0
GLA_HEADS = 4
GLA_DK = 128
GLA_DV = 256
GLA_LOWRANK = 16
GLA_GATE_NORM = 16.0
GLA_CHUNK = 64
D_FF = 2816
EPS = 1e-6

ATTN_Q_W = N_Q_HEADS * HEAD_DIM
ATTN_KV_W = N_KV_HEADS * HEAD_DIM
GLA_QK_W = GLA_HEADS * GLA_DK
GLA_V_W = GLA_HEADS * GLA_DV
IN_WIDTHS = (ATTN_Q_W, ATTN_KV_W, ATTN_KV_W, GLA_QK_W, GLA_QK_W, GLA_V_W, GLA_V_W, 2 * GLA_LOWRANK, 2 * D_MODEL)
LANES = 128
LR_PAD = LANES
V7X_VMEM_CAP = 56 * 1024 * 1024


def _vmem_limit(nbytes):
    return int(min(max(nbytes, 16 * 1024 * 1024), V7X_VMEM_CAP))


def _const_spec(shape):
    nd = len(shape)
    return pl.BlockSpec(shape, lambda *_: (0,) * nd, pipeline_mode=pl.Buffered(1))


def _rms(x):
    return x * lax.rsqrt(jnp.mean(x * x, axis=-1, keepdims=True) + EPS)


def _sigmoid(x):
    return 1.0 / (1.0 + jnp.exp(-x))


def _log_sigmoid(z):
    return jnp.minimum(z, 0.0) - jnp.log1p(jnp.exp(-jnp.abs(z)))


def _dot(a, b):
    return jnp.dot(a, b, preferred_element_type=F32)


def _dot_nt(a, b):
    return lax.dot_general(a, b, (((1,), (1,)), ((), ())), preferred_element_type=F32)


def _dot_tn(a, b):
    return lax.dot_general(a, b, (((0,), (0,)), ((), ())), preferred_element_type=F32)


def _adaln_kernel(c_ref, w_ref, b_ref, o_ref):
    c = c_ref[...]
    s = (c * _sigmoid(c)).astype(BF16)
    o_ref[...] = _dot(s, w_ref[...].astype(BF16)) + b_ref[...]


def _adaln(cvecs, w_ada, b_ada):
    rows, d = cvecs.shape
    n = w_ada.shape[1]
    bn = n // 4
    return pl.pallas_call(
        _adaln_kernel,
        grid=(n // bn,),
        in_specs=[
            pl.BlockSpec((rows, d), lambda j: (0, 0)),
            pl.BlockSpec((d, bn), lambda j: (0, j)),
            pl.BlockSpec((1, bn), lambda j: (0, j)),
        ],
        out_specs=pl.BlockSpec((rows, bn), lambda j: (0, j)),
        out_shape=jax.ShapeDtypeStruct((rows, n), F32),
        compiler_params=pltpu.CompilerParams(
            dimension_semantics=("arbitrary",),
            vmem_limit_bytes=_vmem_limit(3 * d * bn * 4 + 8 * rows * bn * 4),
        ),
        name="adaln",
    )(cvecs, w_ada, b_ada.reshape(1, n))


def _modulated_norm(x, g, shift, scale):
    return (_rms(x) * g) * (1.0 + scale) + shift


def _head_norm(acc, g, heads, rope):
    outs = []
    for h in range(heads):
        y = _rms(acc[:, h * HEAD_DIM:(h + 1) * HEAD_DIM]) * g
        if rope is not None:
            cosf, sinf = rope
            y = y * cosf + pltpu.roll(y, HEAD_DIM // 2, 1) * sinf
        outs.append(y.astype(BF16))
    return outs


def _log_decays(hb, wlr_ref, up_ref, upb_ref):
    lr = _dot(hb, wlr_ref[...])
    z = _dot(lr.astype(BF16), up_ref[...]) + upb_ref[...]
    return _log_sigmoid(z) * (1.0 / GLA_GATE_NORM)


def _inproj_latent_kernel(x_ref, sh_ref, sc_ref, g_ref, wqkv_ref, wgla_ref, wlr_ref, wmg_ref, up_ref, upb_ref,
                          qg_ref, kg_ref, cos_ref, sin_ref,
                          q_ref, k_ref, v_ref, gq_ref, gk_ref, gv_ref, sgg_ref, la_ref, smg_ref):
    hb = _modulated_norm(x_ref[0], g_ref[...], sh_ref[0], sc_ref[0]).astype(BF16)
    rope = (cos_ref[...], sin_ref[...])
    q_acc = _dot(hb, wqkv_ref[:, :ATTN_Q_W])
    for h, y in enumerate(_head_norm(q_acc, qg_ref[...], N_Q_HEADS, rope)):
        q_ref[0, :, h * HEAD_DIM:(h + 1) * HEAD_DIM] = y
    kv_acc = _dot(hb, wqkv_ref[:, ATTN_Q_W:])
    for h, y in enumerate(_head_norm(kv_acc[:, :ATTN_KV_W], kg_ref[...], N_KV_HEADS, rope)):
        k_ref[0, :, h * HEAD_DIM:(h + 1) * HEAD_DIM] = y
    v_ref[0] = kv_acc[:, ATTN_KV_W:].astype(BF16)
    gqk = _dot(hb, wgla_ref[:, :2 * GLA_QK_W])
    gq_ref[0] = gqk[:, :GLA_QK_W].astype(BF16)
    gk_ref[0] = gqk[:, GLA_QK_W:].astype(BF16)
    gv_ref[0] = _dot(hb, wgla_ref[:, 2 * GLA_QK_W:2 * GLA_QK_W + GLA_V_W]).astype(BF16)
    gg = _dot(hb, wgla_ref[:, 2 * GLA_QK_W + GLA_V_W:])
    sgg_ref[0] = (gg * _sigmoid(gg)).astype(BF16)
    la_ref[0] = _log_decays(hb, wlr_ref, up_ref, upb_ref)
    smg_ref[0] = _sigmoid(_dot(hb, wmg_ref[...])).astype(BF16)


def _inproj_context_kernel(x_ref, sh_ref, sc_ref, g_ref, wkv_ref, wgla_ref, wlr_ref, up_ref, upb_ref, kg_ref,
                           k_ref, v_ref, gq_ref, gk_ref, gv_ref, la_ref):
    hb = _modulated_norm(x_ref[0], g_ref[...], sh_ref[0], sc_ref[0]).astype(BF16)
    kv_acc = _dot(hb, wkv_ref[...])
    for h, y in enumerate(_head_norm(kv_acc[:, :ATTN_KV_W], kg_ref[...], N_KV_HEADS, None)):
        k_ref[0, :, h * HEAD_DIM:(h + 1) * HEAD_DIM] = y
    v_ref[0] = kv_acc[:, ATTN_KV_W:].astype(BF16)
    gqk = _dot(hb, wgla_ref[:, :2 * GLA_QK_W])
    gq_ref[0] = gqk[:, :GLA_QK_W].astype(BF16)
    gk_ref[0] = gqk[:, GLA_QK_W:].astype(BF16)
    gv_ref[0] = _dot(hb, wgla_ref[:, 2 * GLA_QK_W:]).astype(BF16)
    la_ref[0] = _log_decays(hb, wlr_ref, up_ref, upb_ref)


def _row_spec(tm, width):
    return pl.BlockSpec((1, tm, width), lambda b, j: (b, j, 0))


def _vec_spec(width):
    return pl.BlockSpec((1, 1, width), lambda b, j: (b, 0, 0))


def _inproj_latent(x, sh, sc, g, wqkv, wgla, wlr, wmg, up, upb, qg, kg, cosf, sinf, tm):
    B, T, D = x.shape
    out_widths = (ATTN_Q_W, ATTN_KV_W, ATTN_KV_W, GLA_QK_W, GLA_QK_W, GLA_V_W, GLA_V_W, 2 * GLA_QK_W, 2 * D_MODEL)
    out_dtypes = (BF16,) * 7 + (F32, BF16)
    weights = (g, wqkv, wgla, wlr, wmg, up, upb, qg, kg)
    w_bytes = sum(int(np.prod(w.shape)) * w.dtype.itemsize for w in weights)
    tile_bytes = tm * D * 4 + sum(tm * w * jnp.dtype(dt).itemsize for w, dt in zip(out_widths, out_dtypes))
    temp_bytes = tm * (D * 6 + 2 * D_MODEL * 4 * 3)
    return pl.pallas_call(
        _inproj_latent_kernel,
        grid=(B, T // tm),
        in_specs=[_row_spec(tm, D), _vec_spec(D), _vec_spec(D)]
        + [_const_spec(w.shape) for w in weights]
        + [pl.BlockSpec((tm, HEAD_DIM), lambda b, j: (j, 0))] * 2,
        out_specs=[_row_spec(tm, w) for w in out_widths],
        out_shape=[jax.ShapeDtypeStruct((B, T, w), dt) for w, dt in zip(out_widths, out_dtypes)],
        compiler_params=pltpu.CompilerParams(
            dimension_semantics=("arbitrary", "arbitrary"),
            vmem_limit_bytes=_vmem_limit(w_bytes + 2 * tile_bytes + temp_bytes),
        ),
        name="inproj_latent",
    )(x, sh, sc, g, wqkv, wgla, wlr, wmg, up, upb, qg, kg, cosf, sinf)


def _inproj_context(ctx, sh, sc, g, wkv, wgla, wlr, up, upb, kg, tm):
    B, Tc, D = ctx.shape
    out_widths = (ATTN_KV_W, ATTN_KV_W, GLA_QK_W, GLA_QK_W, GLA_V_W, 2 * GLA_QK_W)
    out_dtypes = (BF16,) * 5 + (F32,)
    weights = (g, wkv, wgla, wlr, up, upb, kg)
    w_bytes = sum(int(np.prod(w.shape)) * w.dtype.itemsize for w in weights)
    tile_bytes = tm * D * 4 + sum(tm * w * jnp.dtype(dt).itemsize for w, dt in zip(out_widths, out_dtypes))
    temp_bytes = tm * (D * 6 + 2 * D_MODEL * 4 * 3)
    return pl.pallas_call(
        _inproj_context_kernel,
        grid=(B, Tc // tm),
        in_specs=[_row_spec(tm, D), _vec_spec(D), _vec_spec(D)] + [_const_spec(w.shape) for w in weights],
        out_specs=[_row_spec(tm, w) for w in out_widths],
        out_shape=[jax.ShapeDtypeStruct((B, Tc, w), dt) for w, dt in zip(out_widths, out_dtypes)],
        compiler_params=pltpu.CompilerParams(
            dimension_semantics=("arbitrary", "arbitrary"),
            vmem_limit_bytes=_vmem_limit(w_bytes + 2 * tile_bytes + temp_bytes),
        ),
        name="inproj_context",
    )(ctx, sh, sc, g, wkv, wgla, wlr, up, upb, kg)


def _attn_kernel(q_ref, kc_ref, vc_ref, kx_ref, vx_ref, o_ref):
    q = q_ref[0]
    tq = q.shape[0]
    qs = jnp.concatenate([q[:, g * HEAD_DIM:(g + 1) * HEAD_DIM] for g in range(Q_PER_KV)], axis=0)
    sc = _dot_nt(qs, kc_ref[0])
    sx = _dot_nt(qs, kx_ref[0])
    m = jnp.maximum(jnp.max(sc, axis=-1, keepdims=True), jnp.max(sx, axis=-1, keepdims=True))
    pc = jnp.exp(sc - m)
    px = jnp.exp(sx - m)
    denom = jnp.sum(pc, axis=-1, keepdims=True) + jnp.sum(px, axis=-1, keepdims=True)
    o = _dot(pc.astype(BF16), vc_ref[0]) + _dot(px.astype(BF16), vx_ref[0])
    o = (o / denom).astype(BF16)
    for g in range(Q_PER_KV):
        o_ref[0, :, g * HEAD_DIM:(g + 1) * HEAD_DIM] = o[g * tq:(g + 1) * tq]


def _attention(q, k_c, v_c, k_x, v_x, tq):
    B, T, _ = q.shape
    Tc = k_c.shape[1]
    gw = Q_PER_KV * HEAD_DIM
    score_bytes = Q_PER_KV * tq * (T + Tc) * (4 + 4 + 2)
    io_bytes = 2 * (2 * tq * gw * 2 + 2 * (T + Tc) * HEAD_DIM * 2)
    return pl.pallas_call(
        _attn_kernel,
        grid=(B, N_KV_HEADS, T // tq),
        in_specs=[
            pl.BlockSpec((1, tq, gw), lambda b, h, j: (b, j, h)),
            pl.BlockSpec((1, Tc, HEAD_DIM), lambda b, h, j: (b, 0, h)),
            pl.BlockSpec((1, Tc, HEAD_DIM), lambda b, h, j: (b, 0, h)),
            pl.BlockSpec((1, T, HEAD_DIM), lambda b, h, j: (b, 0, h)),
            pl.BlockSpec((1, T, HEAD_DIM), lambda b, h, j: (b, 0, h)),
        ],
        out_specs=pl.BlockSpec((1, tq, gw), lambda b, h, j: (b, j, h)),
        out_shape=jax.ShapeDtypeStruct((B, T, ATTN_Q_W), BF16),
        compiler_params=pltpu.CompilerParams(
            dimension_semantics=("arbitrary", "arbitrary", "arbitrary"),
            vmem_limit_bytes=_vmem_limit(score_bytes + io_bytes),
        ),
        name="attention",
    )(q, k_c, v_c, k_x, v_x)


def _cumsum_rows(x):
    n = x.shape[0]
    row = lax.broadcasted_iota(jnp.int32, x.shape, 0)
    s = 1
    while s < n:
        x = x + jnp.where(row >= s, pltpu.roll(x, s, 0), 0.0)
        s *= 2
    return x


def _gla_chunk(q, k, v, la, s_ref, reverse, want_out):
    C = q.shape[0]
    b = _cumsum_rows(la)
    tot = b[C - 1:C, :]
    if reverse:
        b = tot - b + la
    kf = k.astype(F32)
    state = s_ref[...]
    out = None
    if want_out:
        qe = (q.astype(F32) * jnp.exp(b) * (GLA_DK ** -0.5)).astype(BF16)
        ke = (kf * jnp.exp(-b)).astype(BF16)
        a = _dot_nt(qe, ke)
        r = lax.broadcasted_iota(jnp.int32, (C, C), 0)
        c = lax.broadcasted_iota(jnp.int32, (C, C), 1)
        a = jnp.where((c >= r) if reverse else (c <= r), a, 0.0)
        out = _dot(a.astype(BF16), v) + _dot(qe, state.astype(BF16))
    kd = (kf * jnp.exp(tot - b)).astype(BF16)
    eye = lax.broadcasted_iota(jnp.int32, (GLA_DK, GLA_DK), 0) == lax.broadcasted_iota(jnp.int32, (GLA_DK, GLA_DK), 1)
    tot_col = jnp.sum(jnp.where(eye, jnp.broadcast_to(tot, (GLA_DK, GLA_DK)), 0.0), axis=1, keepdims=True)
    s_ref[...] = state * jnp.exp(tot_col) + _dot_tn(kd, v)
    return out


def _gla_kernel(qc_ref, kc_ref, vc_ref, lafc_ref, labc_ref, q_ref, k_ref, v_ref, laf_ref, lab_ref, sgg_ref, gn_ref,
                o_ref, s_ref, acc_ref):
    C = GLA_CHUNK
    n_ctx = qc_ref.shape[1] // C
    n_lat = q_ref.shape[1] // C
    s_ref[...] = jnp.zeros_like(s_ref)

    def rows(i):
        return pl.ds(pl.multiple_of(i * C, C), C)

    def ctx_body(i, carry):
        f, r = rows(i), rows(n_ctx - 1 - i)
        _gla_chunk(qc_ref[0, f, :], kc_ref[0, f, :], vc_ref[0, f, :], lafc_ref[0, f, :], s_ref.at[0], False, False)
        _gla_chunk(qc_ref[0, r, :], kc_ref[0, r, :], vc_ref[0, r, :], labc_ref[0, r, :], s_ref.at[1], True, False)
        return carry

    lax.fori_loop(0, n_ctx, ctx_body, 0)

    def lat_body(accumulate, i, carry):
        f, r = rows(i), rows(n_lat - 1 - i)
        of = _gla_chunk(q_ref[0, f, :], k_ref[0, f, :], v_ref[0, f, :], laf_ref[0, f, :], s_ref.at[0], False, True)
        ob = _gla_chunk(q_ref[0, r, :], k_ref[0, r, :], v_ref[0, r, :], lab_ref[0, r, :], s_ref.at[1], True, True)
        if accumulate:
            acc_ref[f, :] += of
            acc_ref[r, :] += ob
        else:
            acc_ref[f, :] = of
            acc_ref[r, :] = ob
        return carry

    lax.fori_loop(0, n_lat // 2, functools.partial(lat_body, False), 0)
    lax.fori_loop(n_lat // 2, n_lat, functools.partial(lat_body, True), 0)

    go = _rms(acc_ref[...]) * gn_ref[...] * sgg_ref[0].astype(F32)
    o_ref[0] = go.astype(BF16)


def _gla(gq_c, gk_c, gv_c, la_c, gq, gk, gv, la, sgg, gn):
    B, T, _ = gq.shape
    Tc = gq_c.shape[1]
    H = GLA_HEADS
    assert (T // GLA_CHUNK) % 2 == 0

    def blk(t, w, off=0):
        return pl.BlockSpec((1, t, w), lambda b, h: (b, 0, h + off))

    io_bytes = 2 * ((T + Tc) * (2 * GLA_DK * 2 + GLA_DV * 2 + 2 * GLA_DK * 4) + 2 * T * GLA_DV * 2)
    scratch_bytes = 2 * GLA_DK * GLA_DV * 4 + T * GLA_DV * 4
    return pl.pallas_call(
        _gla_kernel,
        grid=(B, H),
        in_specs=[
            blk(Tc, GLA_DK), blk(Tc, GLA_DK), blk(Tc, GLA_DV), blk(Tc, GLA_DK), blk(Tc, GLA_DK, H),
            blk(T, GLA_DK), blk(T, GLA_DK), blk(T, GLA_DV), blk(T, GLA_DK), blk(T, GLA_DK, H),
            blk(T, GLA_DV),
            pl.BlockSpec((1, GLA_DV), lambda b, h: (0, 0)),
        ],
        out_specs=blk(T, GLA_DV),
        out_shape=jax.ShapeDtypeStruct((B, T, GLA_V_W), BF16),
        scratch_shapes=[pltpu.VMEM((2, GLA_DK, GLA_DV), F32), pltpu.VMEM((T, GLA_DV), F32)],
        compiler_params=pltpu.CompilerParams(
            dimension_semantics=("arbitrary", "arbitrary"),
            vmem_limit_bytes=_vmem_limit(io_bytes + scratch_bytes + 3 * T * GLA_DV * 4),
        ),
        name="gla",
    )(gq_c, gk_c, gv_c, la_c, la_c, gq, gk, gv, la, la, sgg, gn)


def _merge_ffn_kernel(attn_ref, go_ref, smg_ref, x_ref, g1_ref, sh2_ref, sc2_ref, g2_ref,
                      n2_ref, wa_ref, wg_ref, wo_ref, w1_ref, w2_ref, o_ref):
    ya = _dot(attn_ref[0], wa_ref[...])
    yg = _dot(go_ref[0], wg_ref[...])
    smg = smg_ref[0]
    merged = smg[:, :D_MODEL].astype(F32) * ya + smg[:, D_MODEL:].astype(F32) * yg
    x1 = x_ref[0] + g1_ref[0] * _dot(merged.astype(BF16), wo_ref[...])
    h2 = _modulated_norm(x1, n2_ref[...], sh2_ref[0], sc2_ref[0]).astype(BF16)
    a = _dot(h2, w1_ref[:, :D_FF])
    b = _dot(h2, w1_ref[:, D_FF:])
    u = (a * _sigmoid(a) * b).astype(BF16)
    o_ref[0] = x1 + g2_ref[0] * _dot(u, w2_ref[...])


def _merge_ffn(attn, go, smg, x, g1, sh2, sc2, g2, n2, wa, wg, wo, w1, w2, tm):
    B, T, D = x.shape
    weights = (n2, wa, wg, wo, w1, w2)
    w_bytes = sum(int(np.prod(w.shape)) * w.dtype.itemsize for w in weights)
    tile_bytes = tm * (D * 2 + D * 2 + 2 * D * 2 + D * 4 + D * 4)
    temp_bytes = tm * (2 * D_FF * 4 + D_FF * 2 + 6 * D * 4)
    return pl.pallas_call(
        _merge_ffn_kernel,
        grid=(B, T // tm),
        in_specs=[_row_spec(tm, D), _row_spec(tm, D), _row_spec(tm, 2 * D), _row_spec(tm, D)]
        + [_vec_spec(D)] * 4 + [_const_spec(w.shape) for w in weights],
        out_specs=_row_spec(tm, D),
        out_shape=jax.ShapeDtypeStruct((B, T, D), F32),
        compiler_params=pltpu.CompilerParams(
            dimension_semantics=("arbitrary", "arbitrary"),
            vmem_limit_bytes=_vmem_limit(w_bytes + 2 * tile_bytes + temp_bytes),
        ),
        name="merge_ffn",
    )(attn, go, smg, x, g1, sh2, sc2, g2, n2, wa, wg, wo, w1, w2)


def _deinterleave_perm(heads):
    one = np.concatenate([np.arange(0, HEAD_DIM, 2), np.arange(1, HEAD_DIM, 2)])
    return np.concatenate([h * HEAD_DIM + one for h in range(heads)])


def _rope_tables(T):
    rows = T // GRID_W
    row = jnp.repeat(jnp.arange(rows, dtype=F32), GRID_W)
    col = jnp.tile(jnp.arange(GRID_W, dtype=F32), rows)
    half = HEAD_DIM // 2
    inv_freq = 1.0 / (ROPE_THETA ** (jnp.arange(0, half, 2, dtype=F32) / half))
    ang = jnp.concatenate([row[:, None] * inv_freq[None], col[:, None] * inv_freq[None]], axis=-1)
    cos, sin = jnp.cos(ang), jnp.sin(ang)
    return jnp.concatenate([cos, cos], axis=-1), jnp.concatenate([-sin, sin], axis=-1)


def _block(x, c, ctx, c_ctx, w_ada, b_ada, norm1_g, w_in, q_norm_g, k_norm_g, gk_up_f, gk_up_f_b, gk_up_b, gk_up_b_b,
           gla_norm_g, w_attn_proj, w_gla_proj, w_out, norm2_g, w_ffn_in, w_ffn_out):
    B, T, D = x.shape
    Tc = ctx.shape[1]

    mod = _adaln(jnp.concatenate([c, c_ctx[None, :]], axis=0), w_ada, b_ada)
    sh1, sc1, g1, sh2, sc2, g2 = [m[:B, None, :] for m in jnp.split(mod, 6, axis=-1)]
    sh1c, sc1c = [jnp.broadcast_to(m[B:, None, :], (B, 1, D)) for m in jnp.split(mod, 6, axis=-1)[:2]]

    offs = np.concatenate([[0], np.cumsum(IN_WIDTHS)])
    cols = [w_in[:, offs[i]:offs[i + 1]] for i in range(len(IN_WIDTHS))]
    w_q = cols[0][:, _deinterleave_perm(N_Q_HEADS)]
    w_k = cols[1][:, _deinterleave_perm(N_KV_HEADS)]
    wqkv = jnp.concatenate([w_q, w_k, cols[2]], axis=1).astype(BF16)
    wkv = wqkv[:, ATTN_Q_W:]
    wgla = jnp.concatenate(cols[3:7], axis=1).astype(BF16)
    wgla_c = wgla[:, :2 * GLA_QK_W + GLA_V_W]
    wlr = jnp.pad(cols[7], ((0, 0), (0, LR_PAD - 2 * GLA_LOWRANK))).astype(BF16)
    wmg = cols[8].astype(BF16)
    up = jnp.zeros((LR_PAD, 2 * GLA_QK_W), F32)
    up = up.at[:GLA_LOWRANK, :GLA_QK_W].set(gk_up_f).at[GLA_LOWRANK:2 * GLA_LOWRANK, GLA_QK_W:].set(gk_up_b).astype(BF16)
    upb = jnp.concatenate([gk_up_f_b, gk_up_b_b])[None, :]
    perm1 = _deinterleave_perm(1)
    qg = (q_norm_g[perm1] * (HEAD_DIM ** -0.5))[None, :]
    kg = k_norm_g[perm1][None, :]
    n1 = norm1_g[None, :]
    cosf, sinf = _rope_tables(T)

    k_c, v_c, gq_c, gk_c, gv_c, la_c = _inproj_context(ctx, sh1c, sc1c, n1, wkv, wgla_c, wlr, up, upb, kg, tm=min(Tc, 256))
    q, k_x, v_x, gq, gk, gv, sgg, la, smg = _inproj_latent(
        x, sh1, sc1, n1, wqkv, wgla, wlr, wmg, up, upb, qg, kg, cosf, sinf, tm=min(T, 512))

    attn = _attention(q, k_c, v_c, k_x, v_x, tq=min(T, 256))
    go = _gla(gq_c, gk_c, gv_c, la_c, gq, gk, gv, la, sgg, gla_norm_g[None, :])
    return _merge_ffn(attn, go, smg, x, g1, sh2, sc2, g2, norm2_g[None, :], w_attn_proj.astype(BF16),
                      w_gla_proj.astype(BF16), w_out.astype(BF16), w_ffn_in.astype(BF16), w_ffn_out.astype(BF16),
                      tm=min(T, 256))


def kernel(x, c, ctx, c_ctx, w_ada, b_ada, norm1_g, w_in, q_norm_g, k_norm_g, gk_up_f, gk_up_f_b, gk_up_b, gk_up_b_b,
           gla_norm_g, w_attn_proj, w_gla_proj, w_out, norm2_g, w_ffn_in, w_ffn_out):
    assert w_ada.shape[0] == 1, "single trunk layer"
    return _block(x, c, ctx, c_ctx, w_ada[0], b_ada[0], norm1_g[0], w_in[0], q_norm_g[0], k_norm_g[0], gk_up_f[0],
                  gk_up_f_b[0], gk_up_b[0], gk_up_b_b[0], gla_norm_g[0], w_attn_proj[0], w_gla_proj[0], w_out[0],
                  norm2_g[0], w_ffn_in[0], w_ffn_out[0])
```

```python
import functools

import numpy as np
import jax
import jax.numpy as jnp
from jax import lax
from jax.experimental import pallas as pl
from jax.experimental.pallas import tpu as pltpu

F32 = jnp.float32
BF16 = jnp.bfloat16

D_MODEL = 1024
GRID_W = 64
HEAD_DIM = 128
N_Q_HEADS = 8
N_KV_HEADS = 2
Q_PER_KV = N_Q_HEADS // N_KV_HEADS
ROPE_THETA = 10000.0
GLA_HEADS = 4
GLA_DK = 128
GLA_DV = 256
GLA_LOWRANK = 16
GLA_GATE_NORM = 16.0
GLA_CHUNK = 64
D_FF = 2816
EPS = 1e-6
LOG2_E = 1.4426950408889634
ATTN_ROWS = 256
GLA_UNROLL = 2

ATTN_Q_W = N_Q_HEADS * HEAD_DIM
ATTN_KV_W = N_KV_HEADS * HEAD_DIM
GLA_QK_W = GLA_HEADS * GLA_DK
GLA_V_W = GLA_HEADS * GLA_DV
IN_WIDTHS = (ATTN_Q_W, ATTN_KV_W, ATTN_KV_W, GLA_QK_W, GLA_QK_W, GLA_V_W, GLA_V_W, 2 * GLA_LOWRANK, 2 * D_MODEL)
LANES = 128
LR_PAD = LANES
V7X_VMEM_CAP = 56 * 1024 * 1024


def _vmem_limit(nbytes):
    return int(min(max(nbytes, 16 * 1024 * 1024), V7X_VMEM_CAP))


def _const_spec(shape):
    nd = len(shape)
    return pl.BlockSpec(shape, lambda *_: (0,) * nd, pipeline_mode=pl.Buffered(1))


def _rms(x):
    return x * lax.rsqrt(jnp.mean(x * x, axis=-1, keepdims=True) + EPS)


def _sigmoid(x):
    return 1.0 / (1.0 + jnp.exp(-x))


def _log_sigmoid(z):
    return jnp.minimum(z, 0.0) - jnp.log1p(jnp.exp(-jnp.abs(z)))


def _dot(a, b):
    return jnp.dot(a, b, preferred_element_type=F32)


def _dot_nt(a, b):
    return lax.dot_general(a, b, (((1,), (1,)), ((), ())), preferred_element_type=F32)


def _dot_tn(a, b):
    return lax.dot_general(a, b, (((0,), (0,)), ((), ())), preferred_element_type=F32)


def _adaln_kernel(c_ref, w_ref, b_ref, o_ref):
    c = c_ref[...]
    s = (c * _sigmoid(c)).astype(BF16)
    o_ref[...] = _dot(s, w_ref[...].astype(BF16)) + b_ref[...]


def _adaln(cvecs, w_ada, b_ada):
    rows, d = cvecs.shape
    n = w_ada.shape[1]
    bn = n // 4
    return pl.pallas_call(
        _adaln_kernel,
        grid=(n // bn,),
        in_specs=[
            pl.BlockSpec((rows, d), lambda j: (0, 0)),
            pl.BlockSpec((d, bn), lambda j: (0, j)),
            pl.BlockSpec((1, bn), lambda j: (0, j)),
        ],
        out_specs=pl.BlockSpec((rows, bn), lambda j: (0, j)),
        out_shape=jax.ShapeDtypeStruct((rows, n), F32),
        compiler_params=pltpu.CompilerParams(
            dimension_semantics=("arbitrary",),
            vmem_limit_bytes=_vmem_limit(3 * d * bn * 4 + 8 * rows * bn * 4),
        ),
        name="adaln",
    )(cvecs, w_ada, b_ada.reshape(1, n))


def _modulated_norm(x, g, shift, scale):
    return (_rms(x) * g) * (1.0 + scale) + shift


def _head_norm(acc, g, heads, rope):
    outs = []
    for h in range(heads):
        y = _rms(acc[:, h * HEAD_DIM:(h + 1) * HEAD_DIM]) * g
        if rope is not None:
            cosf, sinf = rope
            y = y * cosf + pltpu.roll(y, HEAD_DIM // 2, 1) * sinf
        outs.append(y.astype(BF16))
    return outs


def _log_decays(hb, wlr_ref, up_ref, upb_ref):
    lr = _dot(hb, wlr_ref[...])
    z = _dot(lr.astype(BF16), up_ref[...]) + upb_ref[...]
    return _log_sigmoid(z) * (1.0 / GLA_GATE_NORM)


def _inproj_latent_kernel(x_ref, sh_ref, sc_ref, g_ref, wqkv_ref, wgla_ref, wlr_ref, wmg_ref, up_ref, upb_ref,
                          qg_ref, kg_ref, cos_ref, sin_ref,
                          q_ref, k_ref, v_ref, gq_ref, gk_ref, gv_ref, sgg_ref, la_ref, smg_ref):
    hb = _modulated_norm(x_ref[0], g_ref[...], sh_ref[0], sc_ref[0]).astype(BF16)
    rope = (cos_ref[...], sin_ref[...])
    q_acc = _dot(hb, wqkv_ref[:, :ATTN_Q_W])
    for h, y in enumerate(_head_norm(q_acc, qg_ref[...], N_Q_HEADS, rope)):
        q_ref[0, :, h * HEAD_DIM:(h + 1) * HEAD_DIM] = y
    kv_acc = _dot(hb, wqkv_ref[:, ATTN_Q_W:])
    for h, y in enumerate(_head_norm(kv_acc[:, :ATTN_KV_W], kg_ref[...], N_KV_HEADS, rope)):
        k_ref[0, :, h * HEAD_DIM:(h + 1) * HEAD_DIM] = y
    v_ref[0] = kv_acc[:, ATTN_KV_W:].astype(BF16)
    gqk = _dot(hb, wgla_ref[:, :2 * GLA_QK_W])
    gq_ref[0] = gqk[:, :GLA_QK_W].astype(BF16)
    gk_ref[0] = gqk[:, GLA_QK_W:].astype(BF16)
    gv_ref[0] = _dot(hb, wgla_ref[:, 2 * GLA_QK_W:2 * GLA_QK_W + GLA_V_W]).astype(BF16)
    gg = _dot(hb, wgla_ref[:, 2 * GLA_QK_W + GLA_V_W:])
    sgg_ref[0] = (gg * _sigmoid(gg)).astype(BF16)
    la_ref[0] = _log_decays(hb, wlr_ref, up_ref, upb_ref)
    smg_ref[0] = _sigmoid(_dot(hb, wmg_ref[...])).astype(BF16)


def _inproj_context_kernel(x_ref, sh_ref, sc_ref, g_ref, wkv_ref, wgla_ref, wlr_ref, up_ref, upb_ref, kg_ref,
                           k_ref, v_ref, gk_ref, gv_ref, la_ref):
    hb = _modulated_norm(x_ref[0], g_ref[...], sh_ref[0], sc_ref[0]).astype(BF16)
    kv_acc = _dot(hb, wkv_ref[...])
    for h, y in enumerate(_head_norm(kv_acc[:, :ATTN_KV_W], kg_ref[...], N_KV_HEADS, None)):
        k_ref[0, :, h * HEAD_DIM:(h + 1) * HEAD_DIM] = y
    v_ref[0] = kv_acc[:, ATTN_KV_W:].astype(BF16)
    gk_ref[0] = _dot(hb, wgla_ref[:, :GLA_QK_W]).astype(BF16)
    gv_ref[0] = _dot(hb, wgla_ref[:, GLA_QK_W:]).astype(BF16)
    la_ref[0] = _log_decays(hb, wlr_ref, up_ref, upb_ref)


def _row_spec(tm, width):
    return pl.BlockSpec((1, tm, width), lambda b, j: (b, j, 0))


def _vec_spec(width):
    return pl.BlockSpec((1, 1, width), lambda b, j: (b, 0, 0))


def _inproj_latent(x, sh, sc, g, wqkv, wgla, wlr, wmg, up, upb, qg, kg, cosf, sinf, tm):
    B, T, D = x.shape
    out_widths = (ATTN_Q_W, ATTN_KV_W, ATTN_KV_W, GLA_QK_W, GLA_QK_W, GLA_V_W, GLA_V_W, 2 * GLA_QK_W, 2 * D_MODEL)
    out_dtypes = (BF16,) * 7 + (F32, BF16)
    weights = (g, wqkv, wgla, wlr, wmg, up, upb, qg, kg)
    w_bytes = sum(int(np.prod(w.shape)) * w.dtype.itemsize for w in weights)
    tile_bytes = tm * D * 4 + sum(tm * w * jnp.dtype(dt).itemsize for w, dt in zip(out_widths, out_dtypes))
    temp_bytes = tm * (D * 6 + 2 * D_MODEL * 4 * 3)
    return pl.pallas_call(
        _inproj_latent_kernel,
        grid=(B, T // tm),
        in_specs=[_row_spec(tm, D), _vec_spec(D), _vec_spec(D)]
        + [_const_spec(w.shape) for w in weights]
        + [pl.BlockSpec((tm, HEAD_DIM), lambda b, j: (j, 0))] * 2,
        out_specs=[_row_spec(tm, w) for w in out_widths],
        out_shape=[jax.ShapeDtypeStruct((B, T, w), dt) for w, dt in zip(out_widths, out_dtypes)],
        compiler_params=pltpu.CompilerParams(
            dimension_semantics=("arbitrary", "arbitrary"),
            vmem_limit_bytes=_vmem_limit(w_bytes + 2 * tile_bytes + temp_bytes),
        ),
        name="inproj_latent",
    )(x, sh, sc, g, wqkv, wgla, wlr, wmg, up, upb, qg, kg, cosf, sinf)


def _inproj_context(ctx, sh, sc, g, wkv, wgla, wlr, up, upb, kg, tm):
    B, Tc, D = ctx.shape
    out_widths = (ATTN_KV_W, ATTN_KV_W, GLA_QK_W, GLA_V_W, 2 * GLA_QK_W)
    out_dtypes = (BF16,) * 4 + (F32,)
    weights = (g, wkv, wgla, wlr, up, upb, kg)
    w_bytes = sum(int(np.prod(w.shape)) * w.dtype.itemsize for w in weights)
    tile_bytes = tm * D * 4 + sum(tm * w * jnp.dtype(dt).itemsize for w, dt in zip(out_widths, out_dtypes))
    temp_bytes = tm * (D * 6 + 2 * D_MODEL * 4 * 3)
    return pl.pallas_call(
        _inproj_context_kernel,
        grid=(B, Tc // tm),
        in_specs=[_row_spec(tm, D), _vec_spec(D), _vec_spec(D)] + [_const_spec(w.shape) for w in weights],
        out_specs=[_row_spec(tm, w) for w in out_widths],
        out_shape=[jax.ShapeDtypeStruct((B, Tc, w), dt) for w, dt in zip(out_widths, out_dtypes)],
        compiler_params=pltpu.CompilerParams(
            dimension_semantics=("arbitrary", "arbitrary"),
            vmem_limit_bytes=_vmem_limit(w_bytes + 2 * tile_bytes + temp_bytes),
        ),
        name="inproj_context",
    )(ctx, sh, sc, g, wkv, wgla, wlr, up, upb, kg)


def _attn_kernel(q_ref, kc_ref, vc_ref, kx_ref, vx_ref, o_ref, k_all, v_aug):
    Tc = kc_ref.shape[1]

    @pl.when(pl.program_id(2) == 0)
    def _():
        k_all[:Tc, :] = kc_ref[0]
        k_all[Tc:, :] = kx_ref[0]
        v_aug[:Tc, :HEAD_DIM] = vc_ref[0]
        v_aug[Tc:, :HEAD_DIM] = vx_ref[0]
        v_aug[:, HEAD_DIM:] = jnp.ones((v_aug.shape[0], HEAD_DIM), BF16)

    for r in range(q_ref.shape[1] // ATTN_ROWS):
        rows = slice(r * ATTN_ROWS, (r + 1) * ATTN_ROWS)
        for g in range(Q_PER_KV):
            lanes = slice(g * HEAD_DIM, (g + 1) * HEAD_DIM)
            s = _dot_nt(q_ref[0, rows, lanes], k_all[...])
            p = jnp.exp2(s - jnp.max(s, axis=-1, keepdims=True)).astype(BF16)
            oa = _dot(p, v_aug[...])
            o_ref[0, rows, lanes] = (oa[:, :HEAD_DIM] / oa[:, HEAD_DIM:]).astype(BF16)


def _attention(q, k_c, v_c, k_x, v_x, tq):
    B, T, _ = q.shape
    Tc = k_c.shape[1]
    gw = Q_PER_KV * HEAD_DIM
    score_bytes = Q_PER_KV * tq * (T + Tc) * (4 + 2)
    io_bytes = 2 * (2 * tq * gw * 2 + 2 * (T + Tc) * HEAD_DIM * 2) + (T + Tc) * 3 * HEAD_DIM * 2
    return pl.pallas_call(
        _attn_kernel,
        grid=(B, N_KV_HEADS, T // tq),
        in_specs=[
            pl.BlockSpec((1, tq, gw), lambda b, h, j: (b, j, h)),
            pl.BlockSpec((1, Tc, HEAD_DIM), lambda b, h, j: (b, 0, h)),
            pl.BlockSpec((1, Tc, HEAD_DIM), lambda b, h, j: (b, 0, h)),
            pl.BlockSpec((1, T, HEAD_DIM), lambda b, h, j: (b, 0, h)),
            pl.BlockSpec((1, T, HEAD_DIM), lambda b, h, j: (b, 0, h)),
        ],
        out_specs=pl.BlockSpec((1, tq, gw), lambda b, h, j: (b, j, h)),
        out_shape=jax.ShapeDtypeStruct((B, T, ATTN_Q_W), BF16),
        scratch_shapes=[pltpu.VMEM((Tc + T, HEAD_DIM), BF16), pltpu.VMEM((Tc + T, 2 * HEAD_DIM), BF16)],
        compiler_params=pltpu.CompilerParams(
            dimension_semantics=("arbitrary", "arbitrary", "arbitrary"),
            vmem_limit_bytes=_vmem_limit(score_bytes + io_bytes),
        ),
        name="attention",
    )(q, k_c, v_c, k_x, v_x)


def _chunk_scan(x, reverse):
    n = x.shape[0]
    pos = lax.broadcasted_iota(jnp.int32, x.shape, 0) & (GLA_CHUNK - 1)
    s = 1
    while s < GLA_CHUNK:
        if reverse:
            x = x + jnp.where(pos < GLA_CHUNK - s, pltpu.roll(x, n - s, 0), 0.0)
        else:
            x = x + jnp.where(pos >= s, pltpu.roll(x, s, 0), 0.0)
        s *= 2
    return x


def _gla_prep(q, k, la, reverse, qe_ref, ke_ref, tot_ref):
    b = _chunk_scan(la, reverse)
    ke_ref[...] = (k.astype(F32) * jnp.exp(-b)).astype(BF16)
    if q is not None:
        qe_ref[...] = (q.astype(F32) * jnp.exp(b) * (GLA_DK ** -0.5)).astype(BF16)
    last = 0 if reverse else GLA_CHUNK - 1
    for n in range(la.shape[0] // GLA_CHUNK):
        row = n * GLA_CHUNK + last
        tot_ref[n:n + 1, :] = b[row:row + 1, :]


def _state_step(state, ke, v, tot):
    eye = lax.broadcasted_iota(jnp.int32, (GLA_DK, GLA_DK), 0) == lax.broadcasted_iota(jnp.int32, (GLA_DK, GLA_DK), 1)
    tot_col = jnp.sum(jnp.where(eye, jnp.broadcast_to(tot, (GLA_DK, GLA_DK)), 0.0), axis=1, keepdims=True)
    return jnp.exp(tot_col) * (state + _dot_tn(ke, v))


def _gla_kernel(kc_ref, vc_ref, lafc_ref, labc_ref, q_ref, k_ref, v_ref, laf_ref, lab_ref, sgg_ref, gn_ref,
                o_ref, acc_ref, qe_ref, ke_ref, tot_ref, kec_ref, totc_ref):
    C = GLA_CHUNK
    n_ctx = kc_ref.shape[1] // C
    n_lat = q_ref.shape[1] // C

    for d, (lac, la) in enumerate(((lafc_ref, laf_ref), (labc_ref, lab_ref))):
        _gla_prep(None, kc_ref[0], lac[0], d == 1, None, kec_ref.at[d], totc_ref.at[d])
        _gla_prep(q_ref[0], k_ref[0], la[0], d == 1, qe_ref.at[d], ke_ref.at[d], tot_ref.at[d])

    def rows(i):
        return slice(i * C, (i + 1) * C)

    r_id = lax.broadcasted_iota(jnp.int32, (C, C), 0)
    c_id = lax.broadcasted_iota(jnp.int32, (C, C), 1)
    for n in range(n_lat):
        r = rows(n)
        a = (jnp.where(c_id <= r_id, _dot_nt(qe_ref[0, r, :], ke_ref[0, r, :]), 0.0)
             + jnp.where(c_id >= r_id, _dot_nt(qe_ref[1, r, :], ke_ref[1, r, :]), 0.0))
        acc_ref[r, :] = _dot(a.astype(BF16), v_ref[0, r, :])

    s_fwd = jnp.zeros((GLA_DK, GLA_DV), F32)
    s_bwd = jnp.zeros((GLA_DK, GLA_DV), F32)
    for i in range(n_ctx):
        j = n_ctx - 1 - i
        s_fwd = _state_step(s_fwd, kec_ref[0, rows(i), :], vc_ref[0, rows(i), :], totc_ref[0, i:i + 1, :])
        s_bwd = _state_step(s_bwd, kec_ref[1, rows(j), :], vc_ref[0, rows(j), :], totc_ref[1, j:j + 1, :])

    for i in range(n_lat):
        j = n_lat - 1 - i
        f, r = rows(i), rows(j)
        acc_ref[f, :] += _dot(qe_ref[0, f, :], s_fwd.astype(BF16))
        s_fwd = _state_step(s_fwd, ke_ref[0, f, :], v_ref[0, f, :], tot_ref[0, i:i + 1, :])
        acc_ref[r, :] += _dot(qe_ref[1, r, :], s_bwd.astype(BF16))
        s_bwd = _state_step(s_bwd, ke_ref[1, r, :], v_ref[0, r, :], tot_ref[1, j:j + 1, :])

    go = _rms(acc_ref[...]) * gn_ref[...] * sgg_ref[0].astype(F32)
    o_ref[0] = go.astype(BF16)


def _gla(gk_c, gv_c, la_c, gq, gk, gv, la, sgg, gn):
    B, T, _ = gq.shape
    Tc = gk_c.shape[1]
    H = GLA_HEADS
    n_lat, n_ctx = T // GLA_CHUNK, Tc // GLA_CHUNK

    def blk(t, w, off=0):
        return pl.BlockSpec((1, t, w), lambda b, h: (b, 0, h + off))

    io_bytes = 2 * (Tc * (GLA_DK * 2 + GLA_DV * 2 + 2 * GLA_DK * 4)
                    + T * (2 * GLA_DK * 2 + GLA_DV * 2 + 2 * GLA_DK * 4 + 2 * GLA_DV * 2))
    scratch = [
        pltpu.VMEM((T, GLA_DV), F32),
        pltpu.VMEM((2, T, GLA_DK), BF16),
        pltpu.VMEM((2, T, GLA_DK), BF16),
        pltpu.VMEM((2, n_lat, GLA_DK), F32),
        pltpu.VMEM((2, Tc, GLA_DK), BF16),
        pltpu.VMEM((2, n_ctx, GLA_DK), F32),
    ]
    scratch_bytes = (2 * GLA_DK * GLA_DV * 4 + T * GLA_DV * 4 + 4 * (T + Tc) * GLA_DK * 2
                     + 2 * (n_lat + n_ctx + 16) * GLA_DK * 4)
    temp_bytes = 6 * T * GLA_DK * 4 + 3 * T * GLA_DV * 4
    return pl.pallas_call(
        _gla_kernel,
        grid=(B, H),
        in_specs=[
            blk(Tc, GLA_DK), blk(Tc, GLA_DV), blk(Tc, GLA_DK), blk(Tc, GLA_DK, H),
            blk(T, GLA_DK), blk(T, GLA_DK), blk(T, GLA_DV), blk(T, GLA_DK), blk(T, GLA_DK, H),
            blk(T, GLA_DV),
            pl.BlockSpec((1, GLA_DV), lambda b, h: (0, 0)),
        ],
        out_specs=blk(T, GLA_DV),
        out_shape=jax.ShapeDtypeStruct((B, T, GLA_V_W), BF16),
        scratch_shapes=scratch,
        compiler_params=pltpu.CompilerParams(
            dimension_semantics=("arbitrary", "arbitrary"),
            vmem_limit_bytes=_vmem_limit(io_bytes + scratch_bytes + temp_bytes),
        ),
        name="gla",
    )(gk_c, gv_c, la_c, la_c, gq, gk, gv, la, la, sgg, gn)


def _merge_ffn_kernel(attn_ref, go_ref, smg_ref, x_ref, g1_ref, sh2_ref, sc2_ref, g2_ref,
                      n2_ref, wa_ref, wg_ref, wo_ref, w1_ref, w2_ref, o_ref):
    ya = _dot(attn_ref[0], wa_ref[...])
    yg = _dot(go_ref[0], wg_ref[...])
    smg = smg_ref[0]
    merged = smg[:, :D_MODEL].astype(F32) * ya + smg[:, D_MODEL:].astype(F32) * yg
    x1 = x_ref[0] + g1_ref[0] * _dot(merged.astype(BF16), wo_ref[...])
    h2 = _modulated_norm(x1, n2_ref[...], sh2_ref[0], sc2_ref[0]).astype(BF16)
    a = _dot(h2, w1_ref[:, :D_FF])
    b = _dot(h2, w1_ref[:, D_FF:])
    u = (a * _sigmoid(a) * b).astype(BF16)
    o_ref[0] = x1 + g2_ref[0] * _dot(u, w2_ref[...])


def _merge_ffn(attn, go, smg, x, g1, sh2, sc2, g2, n2, wa, wg, wo, w1, w2, tm):
    B, T, D = x.shape
    weights = (n2, wa, wg, wo, w1, w2)
    w_bytes = sum(int(np.prod(w.shape)) * w.dtype.itemsize for w in weights)
    tile_bytes = tm * (D * 2 + D * 2 + 2 * D * 2 + D * 4 + D * 4)
    temp_bytes = tm * (2 * D_FF * 4 + D_FF * 2 + 6 * D * 4)
    return pl.pallas_call(
        _merge_ffn_kernel,
        grid=(B, T // tm),
        in_specs=[_row_spec(tm, D), _row_spec(tm, D), _row_spec(tm, 2 * D), _row_spec(tm, D)]
        + [_vec_spec(D)] * 4 + [_const_spec(w.shape) for w in weights],
        out_specs=_row_spec(tm, D),
        out_shape=jax.ShapeDtypeStruct((B, T, D), F32),
        compiler_params=pltpu.CompilerParams(
            dimension_semantics=("arbitrary", "arbitrary"),
            vmem_limit_bytes=_vmem_limit(w_bytes + 2 * tile_bytes + temp_bytes),
        ),
        name="merge_ffn",
    )(attn, go, smg, x, g1, sh2, sc2, g2, n2, wa, wg, wo, w1, w2)


def _deinterleave_perm(heads):
    one = np.concatenate([np.arange(0, HEAD_DIM, 2), np.arange(1, HEAD_DIM, 2)])
    return np.concatenate([h * HEAD_DIM + one for h in range(heads)])


def _rope_tables(T):
    rows = T // GRID_W
    row = jnp.repeat(jnp.arange(rows, dtype=F32), GRID_W)
    col = jnp.tile(jnp.arange(GRID_W, dtype=F32), rows)
    half = HEAD_DIM // 2
    inv_freq = 1.0 / (ROPE_THETA ** (jnp.arange(0, half, 2, dtype=F32) / half))
    ang = jnp.concatenate([row[:, None] * inv_freq[None], col[:, None] * inv_freq[None]], axis=-1)
    cos, sin = jnp.cos(ang), jnp.sin(ang)
    return jnp.concatenate([cos, cos], axis=-1), jnp.concatenate([-sin, sin], axis=-1)


def _block(x, c, ctx, c_ctx, w_ada, b_ada, norm1_g, w_in, q_norm_g, k_norm_g, gk_up_f, gk_up_f_b, gk_up_b, gk_up_b_b,
           gla_norm_g, w_attn_proj, w_gla_proj, w_out, norm2_g, w_ffn_in, w_ffn_out):
    B, T, D = x.shape
    Tc = ctx.shape[1]

    mod = _adaln(jnp.concatenate([c, c_ctx[None, :]], axis=0), w_ada, b_ada)
    sh1, sc1, g1, sh2, sc2, g2 = [m[:B, None, :] for m in jnp.split(mod, 6, axis=-1)]
    sh1c, sc1c = [jnp.broadcast_to(m[B:, None, :], (B, 1, D)) for m in jnp.split(mod, 6, axis=-1)[:2]]

    offs = np.concatenate([[0], np.cumsum(IN_WIDTHS)])
    cols = [w_in[:, offs[i]:offs[i + 1]] for i in range(len(IN_WIDTHS))]
    w_q = cols[0][:, _deinterleave_perm(N_Q_HEADS)]
    w_k = cols[1][:, _deinterleave_perm(N_KV_HEADS)]
    wqkv = jnp.concatenate([w_q, w_k, cols[2]], axis=1).astype(BF16)
    wkv = wqkv[:, ATTN_Q_W:]
    wgla = jnp.concatenate(cols[3:7], axis=1).astype(BF16)
    wgla_c = wgla[:, GLA_QK_W:2 * GLA_QK_W + GLA_V_W]
    wlr = jnp.pad(cols[7], ((0, 0), (0, LR_PAD - 2 * GLA_LOWRANK))).astype(BF16)
    wmg = cols[8].astype(BF16)
    up = jnp.zeros((LR_PAD, 2 * GLA_QK_W), F32)
    up = up.at[:GLA_LOWRANK, :GLA_QK_W].set(gk_up_f).at[GLA_LOWRANK:2 * GLA_LOWRANK, GLA_QK_W:].set(gk_up_b).astype(BF16)
    upb = jnp.concatenate([gk_up_f_b, gk_up_b_b])[None, :]
    perm1 = _deinterleave_perm(1)
    qg = (q_norm_g[perm1] * (HEAD_DIM ** -0.5 * LOG2_E))[None, :]
    kg = k_norm_g[perm1][None, :]
    n1 = norm1_g[None, :]
    cosf, sinf = _rope_tables(T)

    k_c, v_c, gk_c, gv_c, la_c = _inproj_context(ctx, sh1c, sc1c, n1, wkv, wgla_c, wlr, up, upb, kg, tm=min(Tc, 256))
    q, k_x, v_x, gq, gk, gv, sgg, la, smg = _inproj_latent(
        x, sh1, sc1, n1, wqkv, wgla, wlr, wmg, up, upb, qg, kg, cosf, sinf, tm=min(T, 512))

    attn = _attention(q, k_c, v_c, k_x, v_x, tq=min(T, 512))
    go = _gla(gk_c, gv_c, la_c, gq, gk, gv, la, sgg, gla_norm_g[None, :])
    return _merge_ffn(attn, go, smg, x, g1, sh2, sc2, g2, norm2_g[None, :], w_attn_proj.astype(BF16),
                      w_gla_proj.astype(BF16), w_out.astype(BF16), w_ffn_in.astype(BF16), w_ffn_out.astype(BF16),
                      tm=min(T, 256))


def kernel(x, c, ctx, c_ctx, w_ada, b_ada, norm1_g, w_in, q_norm_g, k_norm_g, gk_up_f, gk_up_f_b, gk_up_b, gk_up_b_b,
           gla_norm_g, w_attn_proj, w_gla_proj, w_out, norm2_g, w_ffn_in, w_ffn_out):
    assert w_ada.shape[0] == 1, "single trunk layer"
    return _block(x, c, ctx, c_ctx, w_ada[0], b_ada[0], norm1_g[0], w_in[0], q_norm_g[0], k_norm_g[0], gk_up_f[0],
                  gk_up_f_b[0], gk_up_b[0], gk_up_b_b[0], gla_norm_g[0], w_attn_proj[0], w_gla_proj[0], w_out[0],
                  norm2_g[0], w_ffn_in[0], w_ffn_out[0])
```

```python
import functools

import numpy as np
import jax
import jax.numpy as jnp
from jax import lax
from jax.experimental import pallas as pl
from jax.experimental.pallas import tpu as pltpu

F32 = jnp.float32
BF16 = jnp.bfloat16

D_MODEL = 1024
GRID_W = 64
HEAD_DIM = 128
N_Q_HEADS = 8
N_KV_HEADS = 2
Q_PER_KV = N_Q_HEADS // N_KV_HEADS
ROPE_THETA = 10000.0
GLA_HEADS = 4
GLA_DK = 128
GLA_DV = 256
GLA_LOWRANK = 16
GLA_GATE_NORM = 16.0
GLA_CHUNK = 64
D_FF = 2816
EPS = 1e-6
LOG2_E = 1.4426950408889634
SCAN_ROWS = 256
ATTN_ROWS = 256
GLA_UNROLL = 2

ATTN_Q_W = N_Q_HEADS * HEAD_DIM
ATTN_KV_W = N_KV_HEADS * HEAD_DIM
GLA_QK_W = GLA_HEADS * GLA_DK
GLA_V_W = GLA_HEADS * GLA_DV
IN_WIDTHS = (ATTN_Q_W, ATTN_KV_W, ATTN_KV_W, GLA_QK_W, GLA_QK_W, GLA_V_W, GLA_V_W, 2 * GLA_LOWRANK, 2 * D_MODEL)
LANES = 128
LR_PAD = LANES
V7X_VMEM_CAP = 56 * 1024 * 1024


def _vmem_limit(nbytes):
    return int(min(max(nbytes, 16 * 1024 * 1024), V7X_VMEM_CAP))


def _const_spec(shape):
    nd = len(shape)
    return pl.BlockSpec(shape, lambda *_: (0,) * nd, pipeline_mode=pl.Buffered(1))


def _rms(x):
    return x * lax.rsqrt(jnp.mean(x * x, axis=-1, keepdims=True) + EPS)


def _sigmoid(x):
    return 1.0 / (1.0 + jnp.exp(-x))


def _log_sigmoid(z):
    return jnp.minimum(z, 0.0) - jnp.log1p(jnp.exp(-jnp.abs(z)))


def _dot(a, b):
    return jnp.dot(a, b, preferred_element_type=F32)


def _dot_nt(a, b):
    return lax.dot_general(a, b, (((1,), (1,)), ((), ())), preferred_element_type=F32)


def _dot_tn(a, b):
    return lax.dot_general(a, b, (((0,), (0,)), ((), ())), preferred_element_type=F32)


def _adaln_kernel(c_ref, w_ref, b_ref, o_ref):
    c = c_ref[...]
    s = (c * _sigmoid(c)).astype(BF16)
    o_ref[...] = _dot(s, w_ref[...].astype(BF16)) + b_ref[...]


def _adaln(cvecs, w_ada, b_ada):
    rows, d = cvecs.shape
    n = w_ada.shape[1]
    bn = n // 4
    return pl.pallas_call(
        _adaln_kernel,
        grid=(n // bn,),
        in_specs=[
            pl.BlockSpec((rows, d), lambda j: (0, 0)),
            pl.BlockSpec((d, bn), lambda j: (0, j)),
            pl.BlockSpec((1, bn), lambda j: (0, j)),
        ],
        out_specs=pl.BlockSpec((rows, bn), lambda j: (0, j)),
        out_shape=jax.ShapeDtypeStruct((rows, n), F32),
        compiler_params=pltpu.CompilerParams(
            dimension_semantics=("arbitrary",),
            vmem_limit_bytes=_vmem_limit(3 * d * bn * 4 + 8 * rows * bn * 4),
        ),
        name="adaln",
    )(cvecs, w_ada, b_ada.reshape(1, n))


def _modulated_norm(x, g, shift, scale):
    return (_rms(x) * g) * (1.0 + scale) + shift


def _head_norm(acc, g, heads, rope):
    outs = []
    for h in range(heads):
        y = _rms(acc[:, h * HEAD_DIM:(h + 1) * HEAD_DIM]) * g
        if rope is not None:
            cosf, sinf = rope
            y = y * cosf + pltpu.roll(y, HEAD_DIM // 2, 1) * sinf
        outs.append(y.astype(BF16))
    return outs


def _log_decays(hb, wlr_ref, up_ref, upb_ref):
    lr = _dot(hb, wlr_ref[...])
    z = _dot(lr.astype(BF16), up_ref[...]) + upb_ref[...]
    return _log_sigmoid(z) * (1.0 / GLA_GATE_NORM)


def _inproj_latent_kernel(x_ref, sh_ref, sc_ref, g_ref, wqkv_ref, wgla_ref, wlr_ref, wmg_ref, up_ref, upb_ref,
                          qg_ref, kg_ref, cos_ref, sin_ref,
                          q_ref, k_ref, v_ref, gq_ref, gk_ref, gv_ref, sgg_ref, la_ref, smg_ref):
    hb = _modulated_norm(x_ref[0], g_ref[...], sh_ref[0], sc_ref[0]).astype(BF16)
    rope = (cos_ref[...], sin_ref[...])
    q_acc = _dot(hb, wqkv_ref[:, :ATTN_Q_W])
    for h, y in enumerate(_head_norm(q_acc, qg_ref[...], N_Q_HEADS, rope)):
        q_ref[0, :, h * HEAD_DIM:(h + 1) * HEAD_DIM] = y
    kv_acc = _dot(hb, wqkv_ref[:, ATTN_Q_W:])
    for h, y in enumerate(_head_norm(kv_acc[:, :ATTN_KV_W], kg_ref[...], N_KV_HEADS, rope)):
        k_ref[0, :, h * HEAD_DIM:(h + 1) * HEAD_DIM] = y
    v_ref[0] = kv_acc[:, ATTN_KV_W:].astype(BF16)
    gqk = _dot(hb, wgla_ref[:, :2 * GLA_QK_W])
    gq_ref[0] = gqk[:, :GLA_QK_W].astype(BF16)
    gk_ref[0] = gqk[:, GLA_QK_W:].astype(BF16)
    gv_ref[0] = _dot(hb, wgla_ref[:, 2 * GLA_QK_W:2 * GLA_QK_W + GLA_V_W]).astype(BF16)
    gg = _dot(hb, wgla_ref[:, 2 * GLA_QK_W + GLA_V_W:])
    sgg_ref[0] = (gg * _sigmoid(gg)).astype(BF16)
    la_ref[0] = _log_decays(hb, wlr_ref, up_ref, upb_ref)
    smg_ref[0] = _sigmoid(_dot(hb, wmg_ref[...])).astype(BF16)


def _inproj_context_kernel(x_ref, sh_ref, sc_ref, g_ref, wkv_ref, wgla_ref, wlr_ref, up_ref, upb_ref, kg_ref,
                           k_ref, v_ref, gk_ref, gv_ref, la_ref):
    hb = _modulated_norm(x_ref[0], g_ref[...], sh_ref[0], sc_ref[0]).astype(BF16)
    kv_acc = _dot(hb, wkv_ref[...])
    for h, y in enumerate(_head_norm(kv_acc[:, :ATTN_KV_W], kg_ref[...], N_KV_HEADS, None)):
        k_ref[0, :, h * HEAD_DIM:(h + 1) * HEAD_DIM] = y
    v_ref[0] = kv_acc[:, ATTN_KV_W:].astype(BF16)
    gk_ref[0] = _dot(hb, wgla_ref[:, :GLA_QK_W]).astype(BF16)
    gv_ref[0] = _dot(hb, wgla_ref[:, GLA_QK_W:]).astype(BF16)
    la_ref[0] = _log_decays(hb, wlr_ref, up_ref, upb_ref)


def _row_spec(tm, width):
    return pl.BlockSpec((1, tm, width), lambda b, j: (b, j, 0))


def _vec_spec(width):
    return pl.BlockSpec((1, 1, width), lambda b, j: (b, 0, 0))


def _inproj_latent(x, sh, sc, g, wqkv, wgla, wlr, wmg, up, upb, qg, kg, cosf, sinf, tm):
    B, T, D = x.shape
    out_widths = (ATTN_Q_W, ATTN_KV_W, ATTN_KV_W, GLA_QK_W, GLA_QK_W, GLA_V_W, GLA_V_W, 2 * GLA_QK_W, 2 * D_MODEL)
    out_dtypes = (BF16,) * 7 + (F32, BF16)
    weights = (g, wqkv, wgla, wlr, wmg, up, upb, qg, kg)
    w_bytes = sum(int(np.prod(w.shape)) * w.dtype.itemsize for w in weights)
    tile_bytes = tm * D * 4 + sum(tm * w * jnp.dtype(dt).itemsize for w, dt in zip(out_widths, out_dtypes))
    temp_bytes = tm * (D * 6 + 2 * D_MODEL * 4 * 3)
    return pl.pallas_call(
        _inproj_latent_kernel,
        grid=(B, T // tm),
        in_specs=[_row_spec(tm, D), _vec_spec(D), _vec_spec(D)]
        + [_const_spec(w.shape) for w in weights]
        + [pl.BlockSpec((tm, HEAD_DIM), lambda b, j: (j, 0))] * 2,
        out_specs=[_row_spec(tm, w) for w in out_widths],
        out_shape=[jax.ShapeDtypeStruct((B, T, w), dt) for w, dt in zip(out_widths, out_dtypes)],
        compiler_params=pltpu.CompilerParams(
            dimension_semantics=("arbitrary", "arbitrary"),
            vmem_limit_bytes=_vmem_limit(w_bytes + 2 * tile_bytes + temp_bytes),
        ),
        name="inproj_latent",
    )(x, sh, sc, g, wqkv, wgla, wlr, wmg, up, upb, qg, kg, cosf, sinf)


def _inproj_context(ctx, sh, sc, g, wkv, wgla, wlr, up, upb, kg, tm):
    B, Tc, D = ctx.shape
    out_widths = (ATTN_KV_W, ATTN_KV_W, GLA_QK_W, GLA_V_W, 2 * GLA_QK_W)
    out_dtypes = (BF16,) * 4 + (F32,)
    weights = (g, wkv, wgla, wlr, up, upb, kg)
    w_bytes = sum(int(np.prod(w.shape)) * w.dtype.itemsize for w in weights)
    tile_bytes = tm * D * 4 + sum(tm * w * jnp.dtype(dt).itemsize for w, dt in zip(out_widths, out_dtypes))
    temp_bytes = tm * (D * 6 + 2 * D_MODEL * 4 * 3)
    return pl.pallas_call(
        _inproj_context_kernel,
        grid=(B, Tc // tm),
        in_specs=[_row_spec(tm, D), _vec_spec(D), _vec_spec(D)] + [_const_spec(w.shape) for w in weights],
        out_specs=[_row_spec(tm, w) for w in out_widths],
        out_shape=[jax.ShapeDtypeStruct((B, Tc, w), dt) for w, dt in zip(out_widths, out_dtypes)],
        compiler_params=pltpu.CompilerParams(
            dimension_semantics=("arbitrary", "arbitrary"),
            vmem_limit_bytes=_vmem_limit(w_bytes + 2 * tile_bytes + temp_bytes),
        ),
        name="inproj_context",
    )(ctx, sh, sc, g, wkv, wgla, wlr, up, upb, kg)


def _attn_kernel(q_ref, kc_ref, vc_ref, kx_ref, vx_ref, o_ref, k_all, v_aug):
    Tc = kc_ref.shape[1]

    @pl.when(pl.program_id(2) == 0)
    def _():
        k_all[:Tc, :] = kc_ref[0]
        k_all[Tc:, :] = kx_ref[0]
        v_aug[:Tc, :HEAD_DIM] = vc_ref[0]
        v_aug[Tc:, :HEAD_DIM] = vx_ref[0]
        v_aug[:, HEAD_DIM:] = jnp.ones((v_aug.shape[0], HEAD_DIM), BF16)

    for r in range(q_ref.shape[1] // ATTN_ROWS):
        rows = slice(r * ATTN_ROWS, (r + 1) * ATTN_ROWS)
        for g in range(Q_PER_KV):
            lanes = slice(g * HEAD_DIM, (g + 1) * HEAD_DIM)
            s = _dot_nt(q_ref[0, rows, lanes], k_all[...])
            p = jnp.exp2(s - jnp.max(s, axis=-1, keepdims=True)).astype(BF16)
            oa = _dot(p, v_aug[...])
            o_ref[0, rows, lanes] = (oa[:, :HEAD_DIM] / oa[:, HEAD_DIM:]).astype(BF16)


def _attention(q, k_c, v_c, k_x, v_x, tq):
    B, T, _ = q.shape
    Tc = k_c.shape[1]
    gw = Q_PER_KV * HEAD_DIM
    score_bytes = Q_PER_KV * tq * (T + Tc) * (4 + 2)
    io_bytes = 2 * (2 * tq * gw * 2 + 2 * (T + Tc) * HEAD_DIM * 2) + (T + Tc) * 3 * HEAD_DIM * 2
    return pl.pallas_call(
        _attn_kernel,
        grid=(B, N_KV_HEADS, T // tq),
        in_specs=[
            pl.BlockSpec((1, tq, gw), lambda b, h, j: (b, j, h)),
            pl.BlockSpec((1, Tc, HEAD_DIM), lambda b, h, j: (b, 0, h)),
            pl.BlockSpec((1, Tc, HEAD_DIM), lambda b, h, j: (b, 0, h)),
            pl.BlockSpec((1, T, HEAD_DIM), lambda b, h, j: (b, 0, h)),
            pl.BlockSpec((1, T, HEAD_DIM), lambda b, h, j: (b, 0, h)),
        ],
        out_specs=pl.BlockSpec((1, tq, gw), lambda b, h, j: (b, j, h)),
        out_shape=jax.ShapeDtypeStruct((B, T, ATTN_Q_W), BF16),
        scratch_shapes=[pltpu.VMEM((Tc + T, HEAD_DIM), BF16), pltpu.VMEM((Tc + T, 2 * HEAD_DIM), BF16)],
        compiler_params=pltpu.CompilerParams(
            dimension_semantics=("arbitrary", "arbitrary", "arbitrary"),
            vmem_limit_bytes=_vmem_limit(score_bytes + io_bytes),
        ),
        name="attention",
    )(q, k_c, v_c, k_x, v_x)


def _chunk_scan(x, reverse):
    n, dk = x.shape
    g = min(n, SCAN_ROWS)
    r = lax.broadcasted_iota(jnp.int32, (g, g), 0)
    c = lax.broadcasted_iota(jnp.int32, (g, g), 1)
    same_chunk = (r // GLA_CHUNK) == (c // GLA_CHUNK)
    tri = jnp.where(same_chunk, jnp.where((c >= r) if reverse else (c <= r), 1.0, 0.0), 0.0).astype(BF16)
    hi = x.astype(BF16)
    lo = (x - hi.astype(F32)).astype(BF16)
    parts = jnp.concatenate([hi, lo], axis=1)
    sums = []
    for i in range(n // g):
        y = _dot(tri, parts[i * g:(i + 1) * g, :])
        sums.append(y[:, :dk] + y[:, dk:])
    return jnp.concatenate(sums, axis=0)


def _gla_prep(q, k, la, reverse, qe_ref, ke_ref, tot_ref, first_row):
    b = _chunk_scan(la, reverse)
    ke_ref[...] = (k.astype(F32) * jnp.exp(-b)).astype(BF16)
    if q is not None:
        qe_ref[...] = (q.astype(F32) * jnp.exp(b) * (GLA_DK ** -0.5)).astype(BF16)
    last = 0 if reverse else GLA_CHUNK - 1
    for n in range(la.shape[0] // GLA_CHUNK):
        row = n * GLA_CHUNK + last
        tot_ref[first_row + n:first_row + n + 1, :] = b[row:row + 1, :]


def _gla_kernel(kc_ref, vc_ref, lafc_ref, labc_ref, q_ref, k_ref, v_ref, laf_ref, lab_ref, sgg_ref, gn_ref,
                o_ref, acc_ref, qe_ref, ke_ref, kec_ref, tot_ref, u_ref):
    C = GLA_CHUNK
    n_ctx = kc_ref.shape[1] // C
    n_lat = q_ref.shape[1] // C

    tot_ref[...] = jnp.zeros_like(tot_ref)
    for d, (lac, la) in enumerate(((lafc_ref, laf_ref), (labc_ref, lab_ref))):
        _gla_prep(q_ref[0], k_ref[0], la[0], d == 1, qe_ref.at[d], ke_ref.at[d], tot_ref.at[d], 0)
        _gla_prep(None, kc_ref[0], lac[0], d == 1, None, kec_ref.at[d], tot_ref.at[d], n_lat)
    decay = [jnp.exp(tot_ref[d].T) for d in range(2)]

    def rows(i):
        return slice(i * C, (i + 1) * C)

    r_id = lax.broadcasted_iota(jnp.int32, (C, C), 0)
    c_id = lax.broadcasted_iota(jnp.int32, (C, C), 1)
    scores = []
    for n in range(n_lat):
        r = rows(n)
        a = (jnp.where(c_id <= r_id, _dot_nt(qe_ref[0, r, :], ke_ref[0, r, :]), 0.0)
             + jnp.where(c_id >= r_id, _dot_nt(qe_ref[1, r, :], ke_ref[1, r, :]), 0.0))
        scores.append(a.astype(BF16))

    for d in range(2):
        for n in range(n_lat):
            u_ref[d, n] = _dot_tn(ke_ref[d, rows(n), :], v_ref[0, rows(n), :])
        for n in range(n_ctx):
            u_ref[d, n_lat + n] = _dot_tn(kec_ref[d, rows(n), :], vc_ref[0, rows(n), :])

    for n in range(n_lat):
        acc_ref[rows(n), :] = _dot(scores[n], v_ref[0, rows(n), :])

    def step(state, d, n):
        col = decay[d][:, n:n + 1]
        return col * state + col * u_ref[d, n]

    s_fwd = jnp.zeros((GLA_DK, GLA_DV), F32)
    s_bwd = jnp.zeros((GLA_DK, GLA_DV), F32)
    for i in range(n_ctx):
        s_fwd = step(s_fwd, 0, n_lat + i)
        s_bwd = step(s_bwd, 1, n_lat + n_ctx - 1 - i)

    for i in range(n_lat):
        j = n_lat - 1 - i
        acc_ref[rows(i), :] += _dot(qe_ref[0, rows(i), :], s_fwd.astype(BF16))
        s_fwd = step(s_fwd, 0, i)
        acc_ref[rows(j), :] += _dot(qe_ref[1, rows(j), :], s_bwd.astype(BF16))
        s_bwd = step(s_bwd, 1, j)

    go = _rms(acc_ref[...]) * gn_ref[...] * sgg_ref[0].astype(F32)
    o_ref[0] = go.astype(BF16)


def _gla(gk_c, gv_c, la_c, gq, gk, gv, la, sgg, gn):
    B, T, _ = gq.shape
    Tc = gk_c.shape[1]
    H = GLA_HEADS
    n_lat, n_ctx = T // GLA_CHUNK, Tc // GLA_CHUNK

    def blk(t, w, off=0):
        return pl.BlockSpec((1, t, w), lambda b, h: (b, 0, h + off))

    io_bytes = 2 * (Tc * (GLA_DK * 2 + GLA_DV * 2 + 2 * GLA_DK * 4)
                    + T * (2 * GLA_DK * 2 + GLA_DV * 2 + 2 * GLA_DK * 4 + 2 * GLA_DV * 2))
    scratch = [
        pltpu.VMEM((T, GLA_DV), F32),
        pltpu.VMEM((2, T, GLA_DK), BF16),
        pltpu.VMEM((2, T, GLA_DK), BF16),
        pltpu.VMEM((2, Tc, GLA_DK), BF16),
        pltpu.VMEM((2, GLA_DK, GLA_DK), F32),
        pltpu.VMEM((2, n_lat + n_ctx, GLA_DK, GLA_DV), F32),
    ]
    assert n_lat + n_ctx <= GLA_DK
    scratch_bytes = (T * GLA_DV * 4 + 4 * (T + Tc) * GLA_DK * 2 + 2 * GLA_DK * GLA_DK * 4
                     + 2 * (n_lat + n_ctx) * GLA_DK * GLA_DV * 4)
    temp_bytes = 6 * T * GLA_DK * 4 + 3 * T * GLA_DV * 4
    return pl.pallas_call(
        _gla_kernel,
        grid=(B, H),
        in_specs=[
            blk(Tc, GLA_DK), blk(Tc, GLA_DV), blk(Tc, GLA_DK), blk(Tc, GLA_DK, H),
            blk(T, GLA_DK), blk(T, GLA_DK), blk(T, GLA_DV), blk(T, GLA_DK), blk(T, GLA_DK, H),
            blk(T, GLA_DV),
            pl.BlockSpec((1, GLA_DV), lambda b, h: (0, 0)),
        ],
        out_specs=blk(T, GLA_DV),
        out_shape=jax.ShapeDtypeStruct((B, T, GLA_V_W), BF16),
        scratch_shapes=scratch,
        compiler_params=pltpu.CompilerParams(
            dimension_semantics=("arbitrary", "arbitrary"),
            vmem_limit_bytes=_vmem_limit(io_bytes + scratch_bytes + temp_bytes),
        ),
        name="gla",
    )(gk_c, gv_c, la_c, la_c, gq, gk, gv, la, la, sgg, gn)


def _merge_ffn_kernel(attn_ref, go_ref, smg_ref, x_ref, g1_ref, sh2_ref, sc2_ref, g2_ref,
                      n2_ref, wa_ref, wg_ref, wo_ref, w1_ref, w2_ref, o_ref):
    ya = _dot(attn_ref[0], wa_ref[...])
    yg = _dot(go_ref[0], wg_ref[...])
    smg = smg_ref[0]
    merged = smg[:, :D_MODEL].astype(F32) * ya + smg[:, D_MODEL:].astype(F32) * yg
    x1 = x_ref[0] + g1_ref[0] * _dot(merged.astype(BF16), wo_ref[...])
    h2 = _modulated_norm(x1, n2_ref[...], sh2_ref[0], sc2_ref[0]).astype(BF16)
    a = _dot(h2, w1_ref[:, :D_FF])
    b = _dot(h2, w1_ref[:, D_FF:])
    u = (a * _sigmoid(a) * b).astype(BF16)
    o_ref[0] = x1 + g2_ref[0] * _dot(u, w2_ref[...])


def _merge_ffn(attn, go, smg, x, g1, sh2, sc2, g2, n2, wa, wg, wo, w1, w2, tm):
    B, T, D = x.shape
    weights = (n2, wa, wg, wo, w1, w2)
    w_bytes = sum(int(np.prod(w.shape)) * w.dtype.itemsize for w in weights)
    tile_bytes = tm * (D * 2 + D * 2 + 2 * D * 2 + D * 4 + D * 4)
    temp_bytes = tm * (2 * D_FF * 4 + D_FF * 2 + 6 * D * 4)
    return pl.pallas_call(
        _merge_ffn_kernel,
        grid=(B, T // tm),
        in_specs=[_row_spec(tm, D), _row_spec(tm, D), _row_spec(tm, 2 * D), _row_spec(tm, D)]
        + [_vec_spec(D)] * 4 + [_const_spec(w.shape) for w in weights],
        out_specs=_row_spec(tm, D),
        out_shape=jax.ShapeDtypeStruct((B, T, D), F32),
        compiler_params=pltpu.CompilerParams(
            dimension_semantics=("arbitrary", "arbitrary"),
            vmem_limit_bytes=_vmem_limit(w_bytes + 2 * tile_bytes + temp_bytes),
        ),
        name="merge_ffn",
    )(attn, go, smg, x, g1, sh2, sc2, g2, n2, wa, wg, wo, w1, w2)


def _deinterleave_perm(heads):
    one = np.concatenate([np.arange(0, HEAD_DIM, 2), np.arange(1, HEAD_DIM, 2)])
    return np.concatenate([h * HEAD_DIM + one for h in range(heads)])


def _rope_tables(T):
    rows = T // GRID_W
    row = jnp.repeat(jnp.arange(rows, dtype=F32), GRID_W)
    col = jnp.tile(jnp.arange(GRID_W, dtype=F32), rows)
    half = HEAD_DIM // 2
    inv_freq = 1.0 / (ROPE_THETA ** (jnp.arange(0, half, 2, dtype=F32) / half))
    ang = jnp.concatenate([row[:, None] * inv_freq[None], col[:, None] * inv_freq[None]], axis=-1)
    cos, sin = jnp.cos(ang), jnp.sin(ang)
    return jnp.concatenate([cos, cos], axis=-1), jnp.concatenate([-sin, sin], axis=-1)


def _block(x, c, ctx, c_ctx, w_ada, b_ada, norm1_g, w_in, q_norm_g, k_norm_g, gk_up_f, gk_up_f_b, gk_up_b, gk_up_b_b,
           gla_norm_g, w_attn_proj, w_gla_proj, w_out, norm2_g, w_ffn_in, w_ffn_out):
    B, T, D = x.shape
    Tc = ctx.shape[1]

    mod = _adaln(jnp.concatenate([c, c_ctx[None, :]], axis=0), w_ada, b_ada)
    sh1, sc1, g1, sh2, sc2, g2 = [m[:B, None, :] for m in jnp.split(mod, 6, axis=-1)]
    sh1c, sc1c = [jnp.broadcast_to(m[B:, None, :], (B, 1, D)) for m in jnp.split(mod, 6, axis=-1)[:2]]

    offs = np.concatenate([[0], np.cumsum(IN_WIDTHS)])
    cols = [w_in[:, offs[i]:offs[i + 1]] for i in range(len(IN_WIDTHS))]
    w_q = cols[0][:, _deinterleave_perm(N_Q_HEADS)]
    w_k = cols[1][:, _deinterleave_perm(N_KV_HEADS)]
    wqkv = jnp.concatenate([w_q, w_k, cols[2]], axis=1).astype(BF16)
    wkv = wqkv[:, ATTN_Q_W:]
    wgla = jnp.concatenate(cols[3:7], axis=1).astype(BF16)
    wgla_c = wgla[:, GLA_QK_W:2 * GLA_QK_W + GLA_V_W]
    wlr = jnp.pad(cols[7], ((0, 0), (0, LR_PAD - 2 * GLA_LOWRANK))).astype(BF16)
    wmg = cols[8].astype(BF16)
    up = jnp.zeros((LR_PAD, 2 * GLA_QK_W), F32)
    up = up.at[:GLA_LOWRANK, :GLA_QK_W].set(gk_up_f).at[GLA_LOWRANK:2 * GLA_LOWRANK, GLA_QK_W:].set(gk_up_b).astype(BF16)
    upb = jnp.concatenate([gk_up_f_b, gk_up_b_b])[None, :]
    perm1 = _deinterleave_perm(1)
    qg = (q_norm_g[perm1] * (HEAD_DIM ** -0.5 * LOG2_E))[None, :]
    kg = k_norm_g[perm1][None, :]
    n1 = norm1_g[None, :]
    cosf, sinf = _rope_tables(T)

    k_c, v_c, gk_c, gv_c, la_c = _inproj_context(ctx, sh1c, sc1c, n1, wkv, wgla_c, wlr, up, upb, kg, tm=min(Tc, 256))
    q, k_x, v_x, gq, gk, gv, sgg, la, smg = _inproj_latent(
        x, sh1, sc1, n1, wqkv, wgla, wlr, wmg, up, upb, qg, kg, cosf, sinf, tm=min(T, 512))

    attn = _attention(q, k_c, v_c, k_x, v_x, tq=min(T, 512))
    go = _gla(gk_c, gv_c, la_c, gq, gk, gv, la, sgg, gla_norm_g[None, :])
    return _merge_ffn(attn, go, smg, x, g1, sh2, sc2, g2, norm2_g[None, :], w_attn_proj.astype(BF16),
                      w_gla_proj.astype(BF16), w_out.astype(BF16), w_ffn_in.astype(BF16), w_ffn_out.astype(BF16),
                      tm=min(T, 256))


def kernel(x, c, ctx, c_ctx, w_ada, b_ada, norm1_g, w_in, q_norm_g, k_norm_g, gk_up_f, gk_up_f_b, gk_up_b, gk_up_b_b,
           gla_norm_g, w_attn_proj, w_gla_proj, w_out, norm2_g, w_ffn_in, w_ffn_out):
    assert w_ada.shape[0] == 1, "single trunk layer"
    return _block(x, c, ctx, c_ctx, w_ada[0], b_ada[0], norm1_g[0], w_in[0], q_norm_g[0], k_norm_g[0], gk_up_f[0],
                  gk_up_f_b[0], gk_up_b[0], gk_up_b_b[0], gla_norm_g[0], w_attn_proj[0], w_gla_proj[0], w_out[0],
                  norm2_g[0], w_ffn_in[0], w_ffn_out[0])
```

```python
import functools

import numpy as np
import jax
import jax.numpy as jnp
from jax import lax
from jax.experimental import pallas as pl
from jax.experimental.pallas import tpu as pltpu

F32 = jnp.float32
BF16 = jnp.bfloat16

D_MODEL = 1024
GRID_W = 64
HEAD_DIM = 128
N_Q_HEADS = 8
N_KV_HEADS = 2
Q_PER_KV = N_Q_HEADS // N_KV_HEADS
ROPE_THETA = 10000.0
GLA_HEADS = 4
GLA_DK = 128
GLA_DV = 256
GLA_LOWRANK = 16
GLA_GATE_NORM = 16.0
GLA_CHUNK = 64
D_FF = 2816
EPS = 1e-6
LOG2_E = 1.4426950408889634
INPROJ_ROWS = 256
SCAN_ROWS = 256
ATTN_ROWS = 256
GLA_UNROLL = 2

ATTN_Q_W = N_Q_HEADS * HEAD_DIM
ATTN_KV_W = N_KV_HEADS * HEAD_DIM
GLA_QK_W = GLA_HEADS * GLA_DK
GLA_V_W = GLA_HEADS * GLA_DV
IN_WIDTHS = (ATTN_Q_W, ATTN_KV_W, ATTN_KV_W, GLA_QK_W, GLA_QK_W, GLA_V_W, GLA_V_W, 2 * GLA_LOWRANK, 2 * D_MODEL)
LANES = 128
LR_PAD = LANES
V7X_VMEM_CAP = 56 * 1024 * 1024


def _vmem_limit(nbytes):
    return int(min(max(nbytes, 16 * 1024 * 1024), V7X_VMEM_CAP))


def _const_spec(shape):
    nd = len(shape)
    return pl.BlockSpec(shape, lambda *_: (0,) * nd, pipeline_mode=pl.Buffered(1))


def _rms(x):
    return x * lax.rsqrt(jnp.mean(x * x, axis=-1, keepdims=True) + EPS)


def _sigmoid(x):
    return 1.0 / (1.0 + jnp.exp(-x))


def _log_sigmoid(z):
    return jnp.minimum(z, 0.0) - jnp.log(1.0 + jnp.exp(-jnp.abs(z)))


def _dot(a, b):
    return jnp.dot(a, b, preferred_element_type=F32)


def _dot_nt(a, b):
    return lax.dot_general(a, b, (((1,), (1,)), ((), ())), preferred_element_type=F32)


def _dot_tn(a, b):
    return lax.dot_general(a, b, (((0,), (0,)), ((), ())), preferred_element_type=F32)


def _adaln_kernel(c_ref, w_ref, b_ref, o_ref):
    c = c_ref[...]
    s = (c * _sigmoid(c)).astype(BF16)
    o_ref[...] = _dot(s, w_ref[...].astype(BF16)) + b_ref[...]


def _adaln(cvecs, w_ada, b_ada):
    rows, d = cvecs.shape
    n = w_ada.shape[1]
    bn = n // 4
    return pl.pallas_call(
        _adaln_kernel,
        grid=(n // bn,),
        in_specs=[
            pl.BlockSpec((rows, d), lambda j: (0, 0)),
            pl.BlockSpec((d, bn), lambda j: (0, j)),
            pl.BlockSpec((1, bn), lambda j: (0, j)),
        ],
        out_specs=pl.BlockSpec((rows, bn), lambda j: (0, j)),
        out_shape=jax.ShapeDtypeStruct((rows, n), F32),
        compiler_params=pltpu.CompilerParams(
            dimension_semantics=("arbitrary",),
            vmem_limit_bytes=_vmem_limit(3 * d * bn * 4 + 8 * rows * bn * 4),
        ),
        name="adaln",
    )(cvecs, w_ada, b_ada.reshape(1, n))


def _modulated_norm(x, g, shift, scale):
    return (_rms(x) * g) * (1.0 + scale) + shift


def _head_norm(acc, g, heads, rope):
    outs = []
    for h in range(heads):
        y = _rms(acc[:, h * HEAD_DIM:(h + 1) * HEAD_DIM]) * g
        if rope is not None:
            cosf, sinf = rope
            y = y * cosf + pltpu.roll(y, HEAD_DIM // 2, 1) * sinf
        outs.append(y.astype(BF16))
    return outs


def _log_decays(lr, up_ref, upb_ref):
    z = _dot(lr, up_ref[...]) + upb_ref[...]
    return _log_sigmoid(z) * (1.0 / GLA_GATE_NORM)


def _inproj_latent_kernel(x_ref, sh_ref, sc_ref, g_ref, wqkv_ref, wgla_ref, wlr_ref, wmg_ref, up_ref, upb_ref,
                          qg_ref, kg_ref, cos_ref, sin_ref,
                          q_ref, k_ref, v_ref, gq_ref, gk_ref, gv_ref, sgg_ref, la_ref, smg_ref):
    for r0 in range(0, x_ref.shape[1], INPROJ_ROWS):
        rows = slice(r0, r0 + INPROJ_ROWS)
        hb = _modulated_norm(x_ref[0, rows, :], g_ref[...], sh_ref[0], sc_ref[0]).astype(BF16)
        rope = (cos_ref[rows, :], sin_ref[rows, :])
        lr = _dot(hb, wlr_ref[...]).astype(BF16)
        smg_ref[0, rows, :] = _sigmoid(_dot(hb, wmg_ref[...])).astype(BF16)
        la_ref[0, rows, :] = _log_decays(lr, up_ref, upb_ref)
        gg = _dot(hb, wgla_ref[:, 2 * GLA_QK_W + GLA_V_W:])
        sgg_ref[0, rows, :] = (gg * _sigmoid(gg)).astype(BF16)
        q_acc = _dot(hb, wqkv_ref[:, :ATTN_Q_W])
        for h, y in enumerate(_head_norm(q_acc, qg_ref[...], N_Q_HEADS, rope)):
            q_ref[0, rows, h * HEAD_DIM:(h + 1) * HEAD_DIM] = y
        kv_acc = _dot(hb, wqkv_ref[:, ATTN_Q_W:])
        for h, y in enumerate(_head_norm(kv_acc[:, :ATTN_KV_W], kg_ref[...], N_KV_HEADS, rope)):
            k_ref[0, rows, h * HEAD_DIM:(h + 1) * HEAD_DIM] = y
        v_ref[0, rows, :] = kv_acc[:, ATTN_KV_W:].astype(BF16)
        gqk = _dot(hb, wgla_ref[:, :2 * GLA_QK_W])
        gq_ref[0, rows, :] = gqk[:, :GLA_QK_W].astype(BF16)
        gk_ref[0, rows, :] = gqk[:, GLA_QK_W:].astype(BF16)
        gv_ref[0, rows, :] = _dot(hb, wgla_ref[:, 2 * GLA_QK_W:2 * GLA_QK_W + GLA_V_W]).astype(BF16)


def _inproj_context_kernel(x_ref, sh_ref, sc_ref, g_ref, wkv_ref, wgla_ref, wlr_ref, up_ref, upb_ref, kg_ref,
                           k_ref, v_ref, gk_ref, gv_ref, la_ref):
    hb = _modulated_norm(x_ref[0], g_ref[...], sh_ref[0], sc_ref[0]).astype(BF16)
    lr = _dot(hb, wlr_ref[...]).astype(BF16)
    kv_acc = _dot(hb, wkv_ref[...])
    la_ref[0] = _log_decays(lr, up_ref, upb_ref)
    for h, y in enumerate(_head_norm(kv_acc[:, :ATTN_KV_W], kg_ref[...], N_KV_HEADS, None)):
        k_ref[0, :, h * HEAD_DIM:(h + 1) * HEAD_DIM] = y
    v_ref[0] = kv_acc[:, ATTN_KV_W:].astype(BF16)
    gk_ref[0] = _dot(hb, wgla_ref[:, :GLA_QK_W]).astype(BF16)
    gv_ref[0] = _dot(hb, wgla_ref[:, GLA_QK_W:]).astype(BF16)


def _row_spec(tm, width):
    return pl.BlockSpec((1, tm, width), lambda b, j: (b, j, 0))


def _vec_spec(width):
    return pl.BlockSpec((1, 1, width), lambda b, j: (b, 0, 0))


def _inproj_latent(x, sh, sc, g, wqkv, wgla, wlr, wmg, up, upb, qg, kg, cosf, sinf, tm):
    B, T, D = x.shape
    out_widths = (ATTN_Q_W, ATTN_KV_W, ATTN_KV_W, GLA_QK_W, GLA_QK_W, GLA_V_W, GLA_V_W, 2 * GLA_QK_W, 2 * D_MODEL)
    out_dtypes = (BF16,) * 7 + (F32, BF16)
    weights = (g, wqkv, wgla, wlr, wmg, up, upb, qg, kg)
    w_bytes = sum(int(np.prod(w.shape)) * w.dtype.itemsize for w in weights)
    tile_bytes = tm * D * 4 + sum(tm * w * jnp.dtype(dt).itemsize for w, dt in zip(out_widths, out_dtypes))
    temp_bytes = tm * (D * 6 + 2 * D_MODEL * 4 * 3)
    return pl.pallas_call(
        _inproj_latent_kernel,
        grid=(B, T // tm),
        in_specs=[_row_spec(tm, D), _vec_spec(D), _vec_spec(D)]
        + [_const_spec(w.shape) for w in weights]
        + [pl.BlockSpec((tm, HEAD_DIM), lambda b, j: (j, 0))] * 2,
        out_specs=[_row_spec(tm, w) for w in out_widths],
        out_shape=[jax.ShapeDtypeStruct((B, T, w), dt) for w, dt in zip(out_widths, out_dtypes)],
        compiler_params=pltpu.CompilerParams(
            dimension_semantics=("arbitrary", "arbitrary"),
            vmem_limit_bytes=_vmem_limit(w_bytes + 2 * tile_bytes + temp_bytes),
        ),
        name="inproj_latent",
    )(x, sh, sc, g, wqkv, wgla, wlr, wmg, up, upb, qg, kg, cosf, sinf)


def _inproj_context(ctx, sh, sc, g, wkv, wgla, wlr, up, upb, kg, tm):
    B, Tc, D = ctx.shape
    out_widths = (ATTN_KV_W, ATTN_KV_W, GLA_QK_W, GLA_V_W, 2 * GLA_QK_W)
    out_dtypes = (BF16,) * 4 + (F32,)
    weights = (g, wkv, wgla, wlr, up, upb, kg)
    w_bytes = sum(int(np.prod(w.shape)) * w.dtype.itemsize for w in weights)
    tile_bytes = tm * D * 4 + sum(tm * w * jnp.dtype(dt).itemsize for w, dt in zip(out_widths, out_dtypes))
    temp_bytes = tm * (D * 6 + 2 * D_MODEL * 4 * 3)
    return pl.pallas_call(
        _inproj_context_kernel,
        grid=(B, Tc // tm),
        in_specs=[_row_spec(tm, D), _vec_spec(D), _vec_spec(D)] + [_const_spec(w.shape) for w in weights],
        out_specs=[_row_spec(tm, w) for w in out_widths],
        out_shape=[jax.ShapeDtypeStruct((B, Tc, w), dt) for w, dt in zip(out_widths, out_dtypes)],
        compiler_params=pltpu.CompilerParams(
            dimension_semantics=("arbitrary", "arbitrary"),
            vmem_limit_bytes=_vmem_limit(w_bytes + 2 * tile_bytes + temp_bytes),
        ),
        name="inproj_context",
    )(ctx, sh, sc, g, wkv, wgla, wlr, up, upb, kg)


def _attn_kernel(q_ref, kc_ref, vc_ref, kx_ref, vx_ref, o_ref, k_all, v_aug):
    Tc = kc_ref.shape[1]

    @pl.when(pl.program_id(2) == 0)
    def _():
        k_all[:Tc, :] = kc_ref[0]
        k_all[Tc:, :] = kx_ref[0]
        v_aug[:Tc, :HEAD_DIM] = vc_ref[0]
        v_aug[Tc:, :HEAD_DIM] = vx_ref[0]
        v_aug[:, HEAD_DIM:] = jnp.ones((v_aug.shape[0], HEAD_DIM), BF16)

    chains = [(slice(r * ATTN_ROWS, (r + 1) * ATTN_ROWS), slice(g * HEAD_DIM, (g + 1) * HEAD_DIM))
              for r in range(q_ref.shape[1] // ATTN_ROWS) for g in range(Q_PER_KV)]

    def scores(chain):
        rows, lanes = chain
        return _dot_nt(q_ref[0, rows, lanes], k_all[...])

    s_next = scores(chains[0])
    for i, (rows, lanes) in enumerate(chains):
        s = s_next
        if i + 1 < len(chains):
            s_next = scores(chains[i + 1])
        p = jnp.exp2(s - jnp.max(s, axis=-1, keepdims=True)).astype(BF16)
        oa = _dot(p, v_aug[...])
        o_ref[0, rows, lanes] = (oa[:, :HEAD_DIM] / oa[:, HEAD_DIM:]).astype(BF16)


def _attention(q, k_c, v_c, k_x, v_x, tq):
    B, T, _ = q.shape
    Tc = k_c.shape[1]
    gw = Q_PER_KV * HEAD_DIM
    score_bytes = Q_PER_KV * tq * (T + Tc) * (4 + 2)
    io_bytes = 2 * (2 * tq * gw * 2 + 2 * (T + Tc) * HEAD_DIM * 2) + (T + Tc) * 3 * HEAD_DIM * 2
    return pl.pallas_call(
        _attn_kernel,
        grid=(B, N_KV_HEADS, T // tq),
        in_specs=[
            pl.BlockSpec((1, tq, gw), lambda b, h, j: (b, j, h)),
            pl.BlockSpec((1, Tc, HEAD_DIM), lambda b, h, j: (b, 0, h)),
            pl.BlockSpec((1, Tc, HEAD_DIM), lambda b, h, j: (b, 0, h)),
            pl.BlockSpec((1, T, HEAD_DIM), lambda b, h, j: (b, 0, h)),
            pl.BlockSpec((1, T, HEAD_DIM), lambda b, h, j: (b, 0, h)),
        ],
        out_specs=pl.BlockSpec((1, tq, gw), lambda b, h, j: (b, j, h)),
        out_shape=jax.ShapeDtypeStruct((B, T, ATTN_Q_W), BF16),
        scratch_shapes=[pltpu.VMEM((Tc + T, HEAD_DIM), BF16), pltpu.VMEM((Tc + T, 2 * HEAD_DIM), BF16)],
        compiler_params=pltpu.CompilerParams(
            dimension_semantics=("arbitrary", "arbitrary", "arbitrary"),
            vmem_limit_bytes=_vmem_limit(score_bytes + io_bytes),
        ),
        name="attention",
    )(q, k_c, v_c, k_x, v_x)


def _chunk_scan(x, reverse):
    n, dk = x.shape
    g = min(n, SCAN_ROWS)
    r = lax.broadcasted_iota(jnp.int32, (g, g), 0)
    c = lax.broadcasted_iota(jnp.int32, (g, g), 1)
    same_chunk = (r // GLA_CHUNK) == (c // GLA_CHUNK)
    tri = jnp.where(same_chunk, jnp.where((c >= r) if reverse else (c <= r), 1.0, 0.0), 0.0).astype(BF16)
    hi = x.astype(BF16)
    lo = (x - hi.astype(F32)).astype(BF16)
    parts = jnp.concatenate([hi, lo], axis=1)
    sums = []
    for i in range(n // g):
        y = _dot(tri, parts[i * g:(i + 1) * g, :])
        sums.append(y[:, :dk] + y[:, dk:])
    return jnp.concatenate(sums, axis=0)


def _gla_prep(q, k, la, reverse, qe_ref, ke_ref, tot_ref, first_row):
    b = _chunk_scan(la, reverse)
    ke_ref[...] = (k.astype(F32) * jnp.exp(-b)).astype(BF16)
    if q is not None:
        qe_ref[...] = (q.astype(F32) * jnp.exp(b) * (GLA_DK ** -0.5)).astype(BF16)
    last = 0 if reverse else GLA_CHUNK - 1
    for n in range(la.shape[0] // GLA_CHUNK):
        row = n * GLA_CHUNK + last
        tot_ref[first_row + n:first_row + n + 1, :] = b[row:row + 1, :]


def _gla_kernel(kc_ref, vc_ref, lafc_ref, labc_ref, q_ref, k_ref, v_ref, laf_ref, lab_ref, sgg_ref, gn_ref,
                o_ref, acc_ref, qe_ref, ke_ref, kec_ref, tot_ref, u_ref):
    C = GLA_CHUNK
    n_ctx = kc_ref.shape[1] // C
    n_lat = q_ref.shape[1] // C

    tot_ref[...] = jnp.zeros_like(tot_ref)
    for d, (lac, la) in enumerate(((lafc_ref, laf_ref), (labc_ref, lab_ref))):
        _gla_prep(q_ref[0], k_ref[0], la[0], d == 1, qe_ref.at[d], ke_ref.at[d], tot_ref.at[d], 0)
        _gla_prep(None, kc_ref[0], lac[0], d == 1, None, kec_ref.at[d], tot_ref.at[d], n_lat)
    decay = [jnp.exp(tot_ref[d].T) for d in range(2)]

    def rows(i):
        return slice(i * C, (i + 1) * C)

    r_id = lax.broadcasted_iota(jnp.int32, (C, C), 0)
    c_id = lax.broadcasted_iota(jnp.int32, (C, C), 1)
    scores = []
    for n in range(n_lat):
        r = rows(n)
        a = (jnp.where(c_id <= r_id, _dot_nt(qe_ref[0, r, :], ke_ref[0, r, :]), 0.0)
             + jnp.where(c_id >= r_id, _dot_nt(qe_ref[1, r, :], ke_ref[1, r, :]), 0.0))
        scores.append(a.astype(BF16))

    for d in range(2):
        for n in range(n_lat):
            u_ref[d, n] = _dot_tn(ke_ref[d, rows(n), :], v_ref[0, rows(n), :])
        for n in range(n_ctx):
            u_ref[d, n_lat + n] = _dot_tn(kec_ref[d, rows(n), :], vc_ref[0, rows(n), :])

    for n in range(n_lat):
        acc_ref[rows(n), :] = _dot(scores[n], v_ref[0, rows(n), :])

    def step(state, d, n):
        col = decay[d][:, n:n + 1]
        return col * state + col * u_ref[d, n]

    s_fwd = jnp.zeros((GLA_DK, GLA_DV), F32)
    s_bwd = jnp.zeros((GLA_DK, GLA_DV), F32)
    for i in range(n_ctx):
        s_fwd = step(s_fwd, 0, n_lat + i)
        s_bwd = step(s_bwd, 1, n_lat + n_ctx - 1 - i)

    for i in range(n_lat):
        j = n_lat - 1 - i
        acc_ref[rows(i), :] += _dot(qe_ref[0, rows(i), :], s_fwd.astype(BF16))
        s_fwd = step(s_fwd, 0, i)
        acc_ref[rows(j), :] += _dot(qe_ref[1, rows(j), :], s_bwd.astype(BF16))
        s_bwd = step(s_bwd, 1, j)

    go = _rms(acc_ref[...]) * gn_ref[...] * sgg_ref[0].astype(F32)
    o_ref[0] = go.astype(BF16)


def _gla(gk_c, gv_c, la_c, gq, gk, gv, la, sgg, gn):
    B, T, _ = gq.shape
    Tc = gk_c.shape[1]
    H = GLA_HEADS
    n_lat, n_ctx = T // GLA_CHUNK, Tc // GLA_CHUNK

    def blk(t, w, off=0):
        return pl.BlockSpec((1, t, w), lambda b, h: (b, 0, h + off))

    io_bytes = 2 * (Tc * (GLA_DK * 2 + GLA_DV * 2 + 2 * GLA_DK * 4)
                    + T * (2 * GLA_DK * 2 + GLA_DV * 2 + 2 * GLA_DK * 4 + 2 * GLA_DV * 2))
    scratch = [
        pltpu.VMEM((T, GLA_DV), F32),
        pltpu.VMEM((2, T, GLA_DK), BF16),
        pltpu.VMEM((2, T, GLA_DK), BF16),
        pltpu.VMEM((2, Tc, GLA_DK), BF16),
        pltpu.VMEM((2, GLA_DK, GLA_DK), F32),
        pltpu.VMEM((2, n_lat + n_ctx, GLA_DK, GLA_DV), F32),
    ]
    assert n_lat + n_ctx <= GLA_DK
    scratch_bytes = (T * GLA_DV * 4 + 4 * (T + Tc) * GLA_DK * 2 + 2 * GLA_DK * GLA_DK * 4
                     + 2 * (n_lat + n_ctx) * GLA_DK * GLA_DV * 4)
    temp_bytes = 6 * T * GLA_DK * 4 + 3 * T * GLA_DV * 4
    return pl.pallas_call(
        _gla_kernel,
        grid=(B, H),
        in_specs=[
            blk(Tc, GLA_DK), blk(Tc, GLA_DV), blk(Tc, GLA_DK), blk(Tc, GLA_DK, H),
            blk(T, GLA_DK), blk(T, GLA_DK), blk(T, GLA_DV), blk(T, GLA_DK), blk(T, GLA_DK, H),
            blk(T, GLA_DV),
            pl.BlockSpec((1, GLA_DV), lambda b, h: (0, 0)),
        ],
        out_specs=blk(T, GLA_DV),
        out_shape=jax.ShapeDtypeStruct((B, T, GLA_V_W), BF16),
        scratch_shapes=scratch,
        compiler_params=pltpu.CompilerParams(
            dimension_semantics=("arbitrary", "arbitrary"),
            vmem_limit_bytes=_vmem_limit(io_bytes + scratch_bytes + temp_bytes),
        ),
        name="gla",
    )(gk_c, gv_c, la_c, la_c, gq, gk, gv, la, la, sgg, gn)


def _merge_ffn_kernel(attn_ref, go_ref, smg_ref, x_ref, g1_ref, sh2_ref, sc2_ref, g2_ref,
                      n2_ref, wa_ref, wg_ref, wo_ref, w1_ref, w2_ref, o_ref):
    ya = _dot(attn_ref[0], wa_ref[...])
    yg = _dot(go_ref[0], wg_ref[...])
    smg = smg_ref[0]
    merged = smg[:, :D_MODEL].astype(F32) * ya + smg[:, D_MODEL:].astype(F32) * yg
    x1 = x_ref[0] + g1_ref[0] * _dot(merged.astype(BF16), wo_ref[...])
    h2 = _modulated_norm(x1, n2_ref[...], sh2_ref[0], sc2_ref[0]).astype(BF16)
    a = _dot(h2, w1_ref[:, :D_FF])
    b = _dot(h2, w1_ref[:, D_FF:])
    u = (a * _sigmoid(a) * b).astype(BF16)
    o_ref[0] = x1 + g2_ref[0] * _dot(u, w2_ref[...])


def _merge_ffn(attn, go, smg, x, g1, sh2, sc2, g2, n2, wa, wg, wo, w1, w2, tm):
    B, T, D = x.shape
    weights = (n2, wa, wg, wo, w1, w2)
    w_bytes = sum(int(np.prod(w.shape)) * w.dtype.itemsize for w in weights)
    tile_bytes = tm * (D * 2 + D * 2 + 2 * D * 2 + D * 4 + D * 4)
    temp_bytes = tm * (2 * D_FF * 4 + D_FF * 2 + 6 * D * 4)
    return pl.pallas_call(
        _merge_ffn_kernel,
        grid=(B, T // tm),
        in_specs=[_row_spec(tm, D), _row_spec(tm, D), _row_spec(tm, 2 * D), _row_spec(tm, D)]
        + [_vec_spec(D)] * 4 + [_const_spec(w.shape) for w in weights],
        out_specs=_row_spec(tm, D),
        out_shape=jax.ShapeDtypeStruct((B, T, D), F32),
        compiler_params=pltpu.CompilerParams(
            dimension_semantics=("arbitrary", "arbitrary"),
            vmem_limit_bytes=_vmem_limit(w_bytes + 2 * tile_bytes + temp_bytes),
        ),
        name="merge_ffn",
    )(attn, go, smg, x, g1, sh2, sc2, g2, n2, wa, wg, wo, w1, w2)


def _deinterleave_perm(heads):
    one = np.concatenate([np.arange(0, HEAD_DIM, 2), np.arange(1, HEAD_DIM, 2)])
    return np.concatenate([h * HEAD_DIM + one for h in range(heads)])


def _rope_tables(T):
    rows = T // GRID_W
    row = jnp.repeat(jnp.arange(rows, dtype=F32), GRID_W)
    col = jnp.tile(jnp.arange(GRID_W, dtype=F32), rows)
    half = HEAD_DIM // 2
    inv_freq = 1.0 / (ROPE_THETA ** (jnp.arange(0, half, 2, dtype=F32) / half))
    ang = jnp.concatenate([row[:, None] * inv_freq[None], col[:, None] * inv_freq[None]], axis=-1)
    cos, sin = jnp.cos(ang), jnp.sin(ang)
    return jnp.concatenate([cos, cos], axis=-1), jnp.concatenate([-sin, sin], axis=-1)


def _block(x, c, ctx, c_ctx, w_ada, b_ada, norm1_g, w_in, q_norm_g, k_norm_g, gk_up_f, gk_up_f_b, gk_up_b, gk_up_b_b,
           gla_norm_g, w_attn_proj, w_gla_proj, w_out, norm2_g, w_ffn_in, w_ffn_out):
    B, T, D = x.shape
    Tc = ctx.shape[1]

    mod = _adaln(jnp.concatenate([c, c_ctx[None, :]], axis=0), w_ada, b_ada)
    sh1, sc1, g1, sh2, sc2, g2 = [m[:B, None, :] for m in jnp.split(mod, 6, axis=-1)]
    sh1c, sc1c = [jnp.broadcast_to(m[B:, None, :], (B, 1, D)) for m in jnp.split(mod, 6, axis=-1)[:2]]

    offs = np.concatenate([[0], np.cumsum(IN_WIDTHS)])
    cols = [w_in[:, offs[i]:offs[i + 1]] for i in range(len(IN_WIDTHS))]
    w_q = cols[0][:, _deinterleave_perm(N_Q_HEADS)]
    w_k = cols[1][:, _deinterleave_perm(N_KV_HEADS)]
    wqkv = jnp.concatenate([w_q, w_k, cols[2]], axis=1).astype(BF16)
    wkv = wqkv[:, ATTN_Q_W:]
    wgla = jnp.concatenate(cols[3:7], axis=1).astype(BF16)
    wgla_c = wgla[:, GLA_QK_W:2 * GLA_QK_W + GLA_V_W]
    wlr = jnp.pad(cols[7], ((0, 0), (0, LR_PAD - 2 * GLA_LOWRANK))).astype(BF16)
    wmg = cols[8].astype(BF16)
    up = jnp.zeros((LR_PAD, 2 * GLA_QK_W), F32)
    up = up.at[:GLA_LOWRANK, :GLA_QK_W].set(gk_up_f).at[GLA_LOWRANK:2 * GLA_LOWRANK, GLA_QK_W:].set(gk_up_b).astype(BF16)
    upb = jnp.concatenate([gk_up_f_b, gk_up_b_b])[None, :]
    perm1 = _deinterleave_perm(1)
    qg = (q_norm_g[perm1] * (HEAD_DIM ** -0.5 * LOG2_E))[None, :]
    kg = k_norm_g[perm1][None, :]
    n1 = norm1_g[None, :]
    cosf, sinf = _rope_tables(T)

    k_c, v_c, gk_c, gv_c, la_c = _inproj_context(ctx, sh1c, sc1c, n1, wkv, wgla_c, wlr, up, upb, kg, tm=min(Tc, 256))
    q, k_x, v_x, gq, gk, gv, sgg, la, smg = _inproj_latent(
        x, sh1, sc1, n1, wqkv, wgla, wlr, wmg, up, upb, qg, kg, cosf, sinf, tm=min(T, 512))

    attn = _attention(q, k_c, v_c, k_x, v_x, tq=min(T, 512))
    go = _gla(gk_c, gv_c, la_c, gq, gk, gv, la, sgg, gla_norm_g[None, :])
    return _merge_ffn(attn, go, smg, x, g1, sh2, sc2, g2, norm2_g[None, :], w_attn_proj.astype(BF16),
                      w_gla_proj.astype(BF16), w_out.astype(BF16), w_ffn_in.astype(BF16), w_ffn_out.astype(BF16),
                      tm=min(T, 256))


def kernel(x, c, ctx, c_ctx, w_ada, b_ada, norm1_g, w_in, q_norm_g, k_norm_g, gk_up_f, gk_up_f_b, gk_up_b, gk_up_b_b,
           gla_norm_g, w_attn_proj, w_gla_proj, w_out, norm2_g, w_ffn_in, w_ffn_out):
    assert w_ada.shape[0] == 1, "single trunk layer"
    return _block(x, c, ctx, c_ctx, w_ada[0], b_ada[0], norm1_g[0], w_in[0], q_norm_g[0], k_norm_g[0], gk_up_f[0],
                  gk_up_f_b[0], gk_up_b[0], gk_up_b_b[0], gla_norm_g[0], w_attn_proj[0], w_gla_proj[0], w_out[0],
                  norm2_g[0], w_ffn_in[0], w_ffn_out[0])
```

```python
import functools

import numpy as np
import jax
import jax.numpy as jnp
from jax import lax
from jax.experimental import pallas as pl
from jax.experimental.pallas import tpu as pltpu

F32 = jnp.float32
BF16 = jnp.bfloat16

D_MODEL = 1024
GRID_W = 64
HEAD_DIM = 128
N_Q_HEADS = 8
N_KV_HEADS = 2
Q_PER_KV = N_Q_HEADS // N_KV_HEADS
ROPE_THETA = 10000.0
GLA_HEADS = 4
GLA_DK = 128
GLA_DV = 256
GLA_LOWRANK = 16
GLA_GATE_NORM = 16.0
GLA_CHUNK = 64
D_FF = 2816
EPS = 1e-6
LOG2_E = 1.4426950408889634
MERGE_ROWS = 256
INPROJ_ROWS = 256
SCAN_ROWS = 256
ATTN_ROWS = 256
GLA_UNROLL = 2

ATTN_Q_W = N_Q_HEADS * HEAD_DIM
ATTN_KV_W = N_KV_HEADS * HEAD_DIM
GLA_QK_W = GLA_HEADS * GLA_DK
GLA_V_W = GLA_HEADS * GLA_DV
IN_WIDTHS = (ATTN_Q_W, ATTN_KV_W, ATTN_KV_W, GLA_QK_W, GLA_QK_W, GLA_V_W, GLA_V_W, 2 * GLA_LOWRANK, 2 * D_MODEL)
LANES = 128
LR_PAD = LANES
V7X_VMEM_CAP = 56 * 1024 * 1024


def _vmem_limit(nbytes):
    return int(min(max(nbytes, 16 * 1024 * 1024), V7X_VMEM_CAP))


def _const_spec(shape):
    nd = len(shape)
    return pl.BlockSpec(shape, lambda *_: (0,) * nd, pipeline_mode=pl.Buffered(1))


def _rms(x):
    return x * lax.rsqrt(jnp.mean(x * x, axis=-1, keepdims=True) + EPS)


def _sigmoid(x):
    return 1.0 / (1.0 + jnp.exp(-x))


def _log_sigmoid(z):
    return jnp.minimum(z, 0.0) - jnp.log(1.0 + jnp.exp(-jnp.abs(z)))


def _dot(a, b):
    return jnp.dot(a, b, preferred_element_type=F32)


def _dot_nt(a, b):
    return lax.dot_general(a, b, (((1,), (1,)), ((), ())), preferred_element_type=F32)


def _dot_tn(a, b):
    return lax.dot_general(a, b, (((0,), (0,)), ((), ())), preferred_element_type=F32)


def _adaln_kernel(c_ref, w_ref, b_ref, o_ref):
    c = c_ref[...]
    s = (c * _sigmoid(c)).astype(BF16)
    o_ref[...] = _dot(s, w_ref[...].astype(BF16)) + b_ref[...]


def _adaln(cvecs, w_ada, b_ada):
    rows, d = cvecs.shape
    n = w_ada.shape[1]
    bn = n // 4
    return pl.pallas_call(
        _adaln_kernel,
        grid=(n // bn,),
        in_specs=[
            pl.BlockSpec((rows, d), lambda j: (0, 0)),
            pl.BlockSpec((d, bn), lambda j: (0, j)),
            pl.BlockSpec((1, bn), lambda j: (0, j)),
        ],
        out_specs=pl.BlockSpec((rows, bn), lambda j: (0, j)),
        out_shape=jax.ShapeDtypeStruct((rows, n), F32),
        compiler_params=pltpu.CompilerParams(
            dimension_semantics=("arbitrary",),
            vmem_limit_bytes=_vmem_limit(3 * d * bn * 4 + 8 * rows * bn * 4),
        ),
        name="adaln",
    )(cvecs, w_ada, b_ada.reshape(1, n))


def _modulated_norm(x, g, shift, scale):
    return (_rms(x) * g) * (1.0 + scale) + shift


def _head_norm(acc, g, heads, rope):
    outs = []
    for h in range(heads):
        y = _rms(acc[:, h * HEAD_DIM:(h + 1) * HEAD_DIM]) * g
        if rope is not None:
            cosf, sinf = rope
            y = y * cosf + pltpu.roll(y, HEAD_DIM // 2, 1) * sinf
        outs.append(y.astype(BF16))
    return outs


def _log_decays(lr, up_ref, upb_ref):
    z = _dot(lr, up_ref[...]) + upb_ref[...]
    return _log_sigmoid(z) * (1.0 / GLA_GATE_NORM)


def _inproj_latent_kernel(x_ref, sh_ref, sc_ref, g_ref, wqkv_ref, wgla_ref, wlr_ref, wmg_ref, up_ref, upb_ref,
                          qg_ref, kg_ref, cos_ref, sin_ref,
                          q_ref, k_ref, v_ref, gq_ref, gk_ref, gv_ref, sgg_ref, la_ref, smg_ref):
    for r0 in range(0, x_ref.shape[1], INPROJ_ROWS):
        rows = slice(r0, r0 + INPROJ_ROWS)
        hb = _modulated_norm(x_ref[0, rows, :], g_ref[...], sh_ref[0], sc_ref[0]).astype(BF16)
        rope = (cos_ref[rows, :], sin_ref[rows, :])
        lr = _dot(hb, wlr_ref[...]).astype(BF16)
        smg_ref[0, rows, :] = _sigmoid(_dot(hb, wmg_ref[...])).astype(BF16)
        la_ref[0, rows, :] = _log_decays(lr, up_ref, upb_ref)
        gg = _dot(hb, wgla_ref[:, 2 * GLA_QK_W + GLA_V_W:])
        sgg_ref[0, rows, :] = (gg * _sigmoid(gg)).astype(BF16)
        q_acc = _dot(hb, wqkv_ref[:, :ATTN_Q_W])
        for h, y in enumerate(_head_norm(q_acc, qg_ref[...], N_Q_HEADS, rope)):
            q_ref[0, rows, h * HEAD_DIM:(h + 1) * HEAD_DIM] = y
        kv_acc = _dot(hb, wqkv_ref[:, ATTN_Q_W:])
        for h, y in enumerate(_head_norm(kv_acc[:, :ATTN_KV_W], kg_ref[...], N_KV_HEADS, rope)):
            k_ref[0, rows, h * HEAD_DIM:(h + 1) * HEAD_DIM] = y
        v_ref[0, rows, :] = kv_acc[:, ATTN_KV_W:].astype(BF16)
        gqk = _dot(hb, wgla_ref[:, :2 * GLA_QK_W])
        gq_ref[0, rows, :] = gqk[:, :GLA_QK_W].astype(BF16)
        gk_ref[0, rows, :] = gqk[:, GLA_QK_W:].astype(BF16)
        gv_ref[0, rows, :] = _dot(hb, wgla_ref[:, 2 * GLA_QK_W:2 * GLA_QK_W + GLA_V_W]).astype(BF16)


def _inproj_context_kernel(x_ref, sh_ref, sc_ref, g_ref, wkv_ref, wgla_ref, wlr_ref, up_ref, upb_ref, kg_ref,
                           k_ref, v_ref, gk_ref, gv_ref, la_ref):
    hb = _modulated_norm(x_ref[0], g_ref[...], sh_ref[0], sc_ref[0]).astype(BF16)
    lr = _dot(hb, wlr_ref[...]).astype(BF16)
    kv_acc = _dot(hb, wkv_ref[...])
    la_ref[0] = _log_decays(lr, up_ref, upb_ref)
    for h, y in enumerate(_head_norm(kv_acc[:, :ATTN_KV_W], kg_ref[...], N_KV_HEADS, None)):
        k_ref[0, :, h * HEAD_DIM:(h + 1) * HEAD_DIM] = y
    v_ref[0] = kv_acc[:, ATTN_KV_W:].astype(BF16)
    gk_ref[0] = _dot(hb, wgla_ref[:, :GLA_QK_W]).astype(BF16)
    gv_ref[0] = _dot(hb, wgla_ref[:, GLA_QK_W:]).astype(BF16)


def _row_spec(tm, width):
    return pl.BlockSpec((1, tm, width), lambda b, j: (b, j, 0))


def _vec_spec(width):
    return pl.BlockSpec((1, 1, width), lambda b, j: (b, 0, 0))


def _inproj_latent(x, sh, sc, g, wqkv, wgla, wlr, wmg, up, upb, qg, kg, cosf, sinf, tm):
    B, T, D = x.shape
    out_widths = (ATTN_Q_W, ATTN_KV_W, ATTN_KV_W, GLA_QK_W, GLA_QK_W, GLA_V_W, GLA_V_W, 2 * GLA_QK_W, 2 * D_MODEL)
    out_dtypes = (BF16,) * 7 + (F32, BF16)
    weights = (g, wqkv, wgla, wlr, wmg, up, upb, qg, kg)
    w_bytes = sum(int(np.prod(w.shape)) * w.dtype.itemsize for w in weights)
    tile_bytes = tm * D * 4 + sum(tm * w * jnp.dtype(dt).itemsize for w, dt in zip(out_widths, out_dtypes))
    temp_bytes = tm * (D * 6 + 2 * D_MODEL * 4 * 3)
    return pl.pallas_call(
        _inproj_latent_kernel,
        grid=(B, T // tm),
        in_specs=[_row_spec(tm, D), _vec_spec(D), _vec_spec(D)]
        + [_const_spec(w.shape) for w in weights]
        + [pl.BlockSpec((tm, HEAD_DIM), lambda b, j: (j, 0))] * 2,
        out_specs=[_row_spec(tm, w) for w in out_widths],
        out_shape=[jax.ShapeDtypeStruct((B, T, w), dt) for w, dt in zip(out_widths, out_dtypes)],
        compiler_params=pltpu.CompilerParams(
            dimension_semantics=("arbitrary", "arbitrary"),
            vmem_limit_bytes=_vmem_limit(w_bytes + 2 * tile_bytes + temp_bytes),
        ),
        name="inproj_latent",
    )(x, sh, sc, g, wqkv, wgla, wlr, wmg, up, upb, qg, kg, cosf, sinf)


def _inproj_context(ctx, sh, sc, g, wkv, wgla, wlr, up, upb, kg, tm):
    B, Tc, D = ctx.shape
    out_widths = (ATTN_KV_W, ATTN_KV_W, GLA_QK_W, GLA_V_W, 2 * GLA_QK_W)
    out_dtypes = (BF16,) * 4 + (F32,)
    weights = (g, wkv, wgla, wlr, up, upb, kg)
    w_bytes = sum(int(np.prod(w.shape)) * w.dtype.itemsize for w in weights)
    tile_bytes = tm * D * 4 + sum(tm * w * jnp.dtype(dt).itemsize for w, dt in zip(out_widths, out_dtypes))
    temp_bytes = tm * (D * 6 + 2 * D_MODEL * 4 * 3)
    return pl.pallas_call(
        _inproj_context_kernel,
        grid=(B, Tc // tm),
        in_specs=[_row_spec(tm, D), _vec_spec(D), _vec_spec(D)] + [_const_spec(w.shape) for w in weights],
        out_specs=[_row_spec(tm, w) for w in out_widths],
        out_shape=[jax.ShapeDtypeStruct((B, Tc, w), dt) for w, dt in zip(out_widths, out_dtypes)],
        compiler_params=pltpu.CompilerParams(
            dimension_semantics=("arbitrary", "arbitrary"),
            vmem_limit_bytes=_vmem_limit(w_bytes + 2 * tile_bytes + temp_bytes),
        ),
        name="inproj_context",
    )(ctx, sh, sc, g, wkv, wgla, wlr, up, upb, kg)


def _attn_kernel(q_ref, kc_ref, vc_ref, kx_ref, vx_ref, o_ref, k_all, v_aug):
    Tc = kc_ref.shape[1]

    @pl.when(pl.program_id(2) == 0)
    def _():
        k_all[:Tc, :] = kc_ref[0]
        k_all[Tc:, :] = kx_ref[0]
        v_aug[:Tc, :HEAD_DIM] = vc_ref[0]
        v_aug[Tc:, :HEAD_DIM] = vx_ref[0]
        v_aug[:, HEAD_DIM:] = jnp.ones((v_aug.shape[0], HEAD_DIM), BF16)

    chains = [(slice(r * ATTN_ROWS, (r + 1) * ATTN_ROWS), slice(g * HEAD_DIM, (g + 1) * HEAD_DIM))
              for r in range(q_ref.shape[1] // ATTN_ROWS) for g in range(Q_PER_KV)]

    def scores(chain):
        rows, lanes = chain
        return _dot_nt(q_ref[0, rows, lanes], k_all[...])

    s_next = scores(chains[0])
    for i, (rows, lanes) in enumerate(chains):
        s = s_next
        if i + 1 < len(chains):
            s_next = scores(chains[i + 1])
        p = jnp.exp2(s - jnp.max(s, axis=-1, keepdims=True)).astype(BF16)
        oa = _dot(p, v_aug[...])
        o_ref[0, rows, lanes] = (oa[:, :HEAD_DIM] / oa[:, HEAD_DIM:]).astype(BF16)


def _attention(q, k_c, v_c, k_x, v_x, tq):
    B, T, _ = q.shape
    Tc = k_c.shape[1]
    gw = Q_PER_KV * HEAD_DIM
    score_bytes = Q_PER_KV * tq * (T + Tc) * (4 + 2)
    io_bytes = 2 * (2 * tq * gw * 2 + 2 * (T + Tc) * HEAD_DIM * 2) + (T + Tc) * 3 * HEAD_DIM * 2
    return pl.pallas_call(
        _attn_kernel,
        grid=(B, N_KV_HEADS, T // tq),
        in_specs=[
            pl.BlockSpec((1, tq, gw), lambda b, h, j: (b, j, h)),
            pl.BlockSpec((1, Tc, HEAD_DIM), lambda b, h, j: (b, 0, h)),
            pl.BlockSpec((1, Tc, HEAD_DIM), lambda b, h, j: (b, 0, h)),
            pl.BlockSpec((1, T, HEAD_DIM), lambda b, h, j: (b, 0, h)),
            pl.BlockSpec((1, T, HEAD_DIM), lambda b, h, j: (b, 0, h)),
        ],
        out_specs=pl.BlockSpec((1, tq, gw), lambda b, h, j: (b, j, h)),
        out_shape=jax.ShapeDtypeStruct((B, T, ATTN_Q_W), BF16),
        scratch_shapes=[pltpu.VMEM((Tc + T, HEAD_DIM), BF16), pltpu.VMEM((Tc + T, 2 * HEAD_DIM), BF16)],
        compiler_params=pltpu.CompilerParams(
            dimension_semantics=("arbitrary", "arbitrary", "arbitrary"),
            vmem_limit_bytes=_vmem_limit(score_bytes + io_bytes),
        ),
        name="attention",
    )(q, k_c, v_c, k_x, v_x)


def _chunk_scan(x, reverse):
    n, dk = x.shape
    g = min(n, SCAN_ROWS)
    r = lax.broadcasted_iota(jnp.int32, (g, g), 0)
    c = lax.broadcasted_iota(jnp.int32, (g, g), 1)
    same_chunk = (r // GLA_CHUNK) == (c // GLA_CHUNK)
    tri = jnp.where(same_chunk, jnp.where((c >= r) if reverse else (c <= r), 1.0, 0.0), 0.0).astype(BF16)
    hi = x.astype(BF16)
    lo = (x - hi.astype(F32)).astype(BF16)
    parts = jnp.concatenate([hi, lo], axis=1)
    sums = []
    for i in range(n // g):
        y = _dot(tri, parts[i * g:(i + 1) * g, :])
        sums.append(y[:, :dk] + y[:, dk:])
    return jnp.concatenate(sums, axis=0)


def _gla_prep(q, k, la, reverse, qe_ref, ke_ref, tot_ref, first_row):
    b = _chunk_scan(la, reverse)
    ke_ref[...] = (k.astype(F32) * jnp.exp(-b)).astype(BF16)
    if q is not None:
        qe_ref[...] = (q.astype(F32) * jnp.exp(b) * (GLA_DK ** -0.5)).astype(BF16)
    last = 0 if reverse else GLA_CHUNK - 1
    for n in range(la.shape[0] // GLA_CHUNK):
        row = n * GLA_CHUNK + last
        tot_ref[first_row + n:first_row + n + 1, :] = b[row:row + 1, :]


def _gla_kernel(kc_ref, vc_ref, lafc_ref, labc_ref, q_ref, k_ref, v_ref, laf_ref, lab_ref, sgg_ref, gn_ref,
                o_ref, acc_ref, qe_ref, ke_ref, kec_ref, tot_ref, u_ref):
    C = GLA_CHUNK
    n_ctx = kc_ref.shape[1] // C
    n_lat = q_ref.shape[1] // C

    tot_ref[...] = jnp.zeros_like(tot_ref)
    for d, (lac, la) in enumerate(((lafc_ref, laf_ref), (labc_ref, lab_ref))):
        _gla_prep(q_ref[0], k_ref[0], la[0], d == 1, qe_ref.at[d], ke_ref.at[d], tot_ref.at[d], 0)
        _gla_prep(None, kc_ref[0], lac[0], d == 1, None, kec_ref.at[d], tot_ref.at[d], n_lat)
    decay = [jnp.exp(tot_ref[d].T) for d in range(2)]

    def rows(i):
        return slice(i * C, (i + 1) * C)

    r_id = lax.broadcasted_iota(jnp.int32, (C, C), 0)
    c_id = lax.broadcasted_iota(jnp.int32, (C, C), 1)
    scores = []
    for n in range(n_lat):
        r = rows(n)
        a = (jnp.where(c_id <= r_id, _dot_nt(qe_ref[0, r, :], ke_ref[0, r, :]), 0.0)
             + jnp.where(c_id >= r_id, _dot_nt(qe_ref[1, r, :], ke_ref[1, r, :]), 0.0))
        scores.append(a.astype(BF16))

    for d in range(2):
        for n in range(n_lat):
            u_ref[d, n] = _dot_tn(ke_ref[d, rows(n), :], v_ref[0, rows(n), :])
        for n in range(n_ctx):
            u_ref[d, n_lat + n] = _dot_tn(kec_ref[d, rows(n), :], vc_ref[0, rows(n), :])

    for n in range(n_lat):
        acc_ref[rows(n), :] = _dot(scores[n], v_ref[0, rows(n), :])

    def step(state, d, n):
        col = decay[d][:, n:n + 1]
        return col * state + col * u_ref[d, n]

    s_fwd = jnp.zeros((GLA_DK, GLA_DV), F32)
    s_bwd = jnp.zeros((GLA_DK, GLA_DV), F32)
    for i in range(n_ctx):
        s_fwd = step(s_fwd, 0, n_lat + i)
        s_bwd = step(s_bwd, 1, n_lat + n_ctx - 1 - i)

    for i in range(n_lat):
        j = n_lat - 1 - i
        acc_ref[rows(i), :] += _dot(qe_ref[0, rows(i), :], s_fwd.astype(BF16))
        s_fwd = step(s_fwd, 0, i)
        acc_ref[rows(j), :] += _dot(qe_ref[1, rows(j), :], s_bwd.astype(BF16))
        s_bwd = step(s_bwd, 1, j)

    go = _rms(acc_ref[...]) * gn_ref[...] * sgg_ref[0].astype(F32)
    o_ref[0] = go.astype(BF16)


def _gla(gk_c, gv_c, la_c, gq, gk, gv, la, sgg, gn):
    B, T, _ = gq.shape
    Tc = gk_c.shape[1]
    H = GLA_HEADS
    n_lat, n_ctx = T // GLA_CHUNK, Tc // GLA_CHUNK

    def blk(t, w, off=0):
        return pl.BlockSpec((1, t, w), lambda b, h: (b, 0, h + off))

    io_bytes = 2 * (Tc * (GLA_DK * 2 + GLA_DV * 2 + 2 * GLA_DK * 4)
                    + T * (2 * GLA_DK * 2 + GLA_DV * 2 + 2 * GLA_DK * 4 + 2 * GLA_DV * 2))
    scratch = [
        pltpu.VMEM((T, GLA_DV), F32),
        pltpu.VMEM((2, T, GLA_DK), BF16),
        pltpu.VMEM((2, T, GLA_DK), BF16),
        pltpu.VMEM((2, Tc, GLA_DK), BF16),
        pltpu.VMEM((2, GLA_DK, GLA_DK), F32),
        pltpu.VMEM((2, n_lat + n_ctx, GLA_DK, GLA_DV), F32),
    ]
    assert n_lat + n_ctx <= GLA_DK
    scratch_bytes = (T * GLA_DV * 4 + 4 * (T + Tc) * GLA_DK * 2 + 2 * GLA_DK * GLA_DK * 4
                     + 2 * (n_lat + n_ctx) * GLA_DK * GLA_DV * 4)
    temp_bytes = 6 * T * GLA_DK * 4 + 3 * T * GLA_DV * 4
    return pl.pallas_call(
        _gla_kernel,
        grid=(B, H),
        in_specs=[
            blk(Tc, GLA_DK), blk(Tc, GLA_DV), blk(Tc, GLA_DK), blk(Tc, GLA_DK, H),
            blk(T, GLA_DK), blk(T, GLA_DK), blk(T, GLA_DV), blk(T, GLA_DK), blk(T, GLA_DK, H),
            blk(T, GLA_DV),
            pl.BlockSpec((1, GLA_DV), lambda b, h: (0, 0)),
        ],
        out_specs=blk(T, GLA_DV),
        out_shape=jax.ShapeDtypeStruct((B, T, GLA_V_W), BF16),
        scratch_shapes=scratch,
        compiler_params=pltpu.CompilerParams(
            dimension_semantics=("arbitrary", "arbitrary"),
            vmem_limit_bytes=_vmem_limit(io_bytes + scratch_bytes + temp_bytes),
        ),
        name="gla",
    )(gk_c, gv_c, la_c, la_c, gq, gk, gv, la, la, sgg, gn)


def _merge_ffn_kernel(attn_ref, go_ref, smg_ref, x_ref, g1_ref, sh2_ref, sc2_ref, g2_ref,
                      n2_ref, wa_ref, wg_ref, wo_ref, w1_ref, w2_ref, o_ref):
    blocks = [slice(r0, r0 + MERGE_ROWS) for r0 in range(0, x_ref.shape[1], MERGE_ROWS)]
    branch = [(_dot(attn_ref[0, r, :], wa_ref[...]), _dot(go_ref[0, r, :], wg_ref[...])) for r in blocks]
    x1 = []
    for r, (ya, yg) in zip(blocks, branch):
        smg = smg_ref[0, r, :]
        merged = smg[:, :D_MODEL].astype(F32) * ya + smg[:, D_MODEL:].astype(F32) * yg
        x1.append(x_ref[0, r, :] + g1_ref[0] * _dot(merged.astype(BF16), wo_ref[...]))
    hidden = []
    for xr in x1:
        h2 = _modulated_norm(xr, n2_ref[...], sh2_ref[0], sc2_ref[0]).astype(BF16)
        hidden.append((_dot(h2, w1_ref[:, :D_FF]), _dot(h2, w1_ref[:, D_FF:])))
    for r, xr, (a, b) in zip(blocks, x1, hidden):
        u = (a * _sigmoid(a) * b).astype(BF16)
        o_ref[0, r, :] = xr + g2_ref[0] * _dot(u, w2_ref[...])


def _merge_ffn(attn, go, smg, x, g1, sh2, sc2, g2, n2, wa, wg, wo, w1, w2, tm):
    B, T, D = x.shape
    weights = (n2, wa, wg, wo, w1, w2)
    w_bytes = sum(int(np.prod(w.shape)) * w.dtype.itemsize for w in weights)
    tile_bytes = tm * (D * 2 + D * 2 + 2 * D * 2 + D * 4 + D * 4)
    temp_bytes = tm * (2 * D_FF * 4 + D_FF * 2 + 6 * D * 4)
    return pl.pallas_call(
        _merge_ffn_kernel,
        grid=(B, T // tm),
        in_specs=[_row_spec(tm, D), _row_spec(tm, D), _row_spec(tm, 2 * D), _row_spec(tm, D)]
        + [_vec_spec(D)] * 4 + [_const_spec(w.shape) for w in weights],
        out_specs=_row_spec(tm, D),
        out_shape=jax.ShapeDtypeStruct((B, T, D), F32),
        compiler_params=pltpu.CompilerParams(
            dimension_semantics=("arbitrary", "arbitrary"),
            vmem_limit_bytes=_vmem_limit(w_bytes + 2 * tile_bytes + temp_bytes),
        ),
        name="merge_ffn",
    )(attn, go, smg, x, g1, sh2, sc2, g2, n2, wa, wg, wo, w1, w2)


def _deinterleave_perm(heads):
    one = np.concatenate([np.arange(0, HEAD_DIM, 2), np.arange(1, HEAD_DIM, 2)])
    return np.concatenate([h * HEAD_DIM + one for h in range(heads)])


def _rope_tables(T):
    rows = T // GRID_W
    row = jnp.repeat(jnp.arange(rows, dtype=F32), GRID_W)
    col = jnp.tile(jnp.arange(GRID_W, dtype=F32), rows)
    half = HEAD_DIM // 2
    inv_freq = 1.0 / (ROPE_THETA ** (jnp.arange(0, half, 2, dtype=F32) / half))
    ang = jnp.concatenate([row[:, None] * inv_freq[None], col[:, None] * inv_freq[None]], axis=-1)
    cos, sin = jnp.cos(ang), jnp.sin(ang)
    return jnp.concatenate([cos, cos], axis=-1), jnp.concatenate([-sin, sin], axis=-1)


def _block(x, c, ctx, c_ctx, w_ada, b_ada, norm1_g, w_in, q_norm_g, k_norm_g, gk_up_f, gk_up_f_b, gk_up_b, gk_up_b_b,
           gla_norm_g, w_attn_proj, w_gla_proj, w_out, norm2_g, w_ffn_in, w_ffn_out):
    B, T, D = x.shape
    Tc = ctx.shape[1]

    mod = _adaln(jnp.concatenate([c, c_ctx[None, :]], axis=0), w_ada, b_ada)
    sh1, sc1, g1, sh2, sc2, g2 = [m[:B, None, :] for m in jnp.split(mod, 6, axis=-1)]
    sh1c, sc1c = [jnp.broadcast_to(m[B:, None, :], (B, 1, D)) for m in jnp.split(mod, 6, axis=-1)[:2]]

    offs = np.concatenate([[0], np.cumsum(IN_WIDTHS)])
    cols = [w_in[:, offs[i]:offs[i + 1]] for i in range(len(IN_WIDTHS))]
    w_q = cols[0][:, _deinterleave_perm(N_Q_HEADS)]
    w_k = cols[1][:, _deinterleave_perm(N_KV_HEADS)]
    wqkv = jnp.concatenate([w_q, w_k, cols[2]], axis=1).astype(BF16)
    wkv = wqkv[:, ATTN_Q_W:]
    wgla = jnp.concatenate(cols[3:7], axis=1).astype(BF16)
    wgla_c = wgla[:, GLA_QK_W:2 * GLA_QK_W + GLA_V_W]
    wlr = jnp.pad(cols[7], ((0, 0), (0, LR_PAD - 2 * GLA_LOWRANK))).astype(BF16)
    wmg = cols[8].astype(BF16)
    up = jnp.zeros((LR_PAD, 2 * GLA_QK_W), F32)
    up = up.at[:GLA_LOWRANK, :GLA_QK_W].set(gk_up_f).at[GLA_LOWRANK:2 * GLA_LOWRANK, GLA_QK_W:].set(gk_up_b).astype(BF16)
    upb = jnp.concatenate([gk_up_f_b, gk_up_b_b])[None, :]
    perm1 = _deinterleave_perm(1)
    qg = (q_norm_g[perm1] * (HEAD_DIM ** -0.5 * LOG2_E))[None, :]
    kg = k_norm_g[perm1][None, :]
    n1 = norm1_g[None, :]
    cosf, sinf = _rope_tables(T)

    k_c, v_c, gk_c, gv_c, la_c = _inproj_context(ctx, sh1c, sc1c, n1, wkv, wgla_c, wlr, up, upb, kg, tm=min(Tc, 256))
    q, k_x, v_x, gq, gk, gv, sgg, la, smg = _inproj_latent(
        x, sh1, sc1, n1, wqkv, wgla, wlr, wmg, up, upb, qg, kg, cosf, sinf, tm=min(T, 512))

    attn = _attention(q, k_c, v_c, k_x, v_x, tq=min(T, 512))
    go = _gla(gk_c, gv_c, la_c, gq, gk, gv, la, sgg, gla_norm_g[None, :])
    return _merge_ffn(attn, go, smg, x, g1, sh2, sc2, g2, norm2_g[None, :], w_attn_proj.astype(BF16),
                      w_gla_proj.astype(BF16), w_out.astype(BF16), w_ffn_in.astype(BF16), w_ffn_out.astype(BF16),
                      tm=min(T, 512))


def kernel(x, c, ctx, c_ctx, w_ada, b_ada, norm1_g, w_in, q_norm_g, k_norm_g, gk_up_f, gk_up_f_b, gk_up_b, gk_up_b_b,
           gla_norm_g, w_attn_proj, w_gla_proj, w_out, norm2_g, w_ffn_in, w_ffn_out):
    assert w_ada.shape[0] == 1, "single trunk layer"
    return _block(x, c, ctx, c_ctx, w_ada[0], b_ada[0], norm1_g[0], w_in[0], q_norm_g[0], k_norm_g[0], gk_up_f[0],
                  gk_up_f_b[0], gk_up_b[0], gk_up_b_b[0], gla_norm_g[0], w_attn_proj[0], w_gla_proj[0], w_out[0],
                  norm2_g[0], w_ffn_in[0], w_ffn_out[0])
```

```python
import functools

import numpy as np
import jax
import jax.numpy as jnp
from jax import lax
from jax.experimental import pallas as pl
from jax.experimental.pallas import tpu as pltpu

F32 = jnp.float32
BF16 = jnp.bfloat16

D_MODEL = 1024
GRID_W = 64
HEAD_DIM = 128
N_Q_HEADS = 8
N_KV_HEADS = 2
Q_PER_KV = N_Q_HEADS // N_KV_HEADS
ROPE_THETA = 10000.0
GLA_HEADS = 4
GLA_DK = 128
GLA_DV = 256
GLA_LOWRANK = 16
GLA_GATE_NORM = 16.0
GLA_CHUNK = 64
D_FF = 2816
EPS = 1e-6
LOG2_E = 1.4426950408889634
MERGE_ROWS = 256
INPROJ_ROWS = 256
SCAN_ROWS = 256
ATTN_ROWS = 256
GLA_UNROLL = 2

ATTN_Q_W = N_Q_HEADS * HEAD_DIM
ATTN_KV_W = N_KV_HEADS * HEAD_DIM
GLA_QK_W = GLA_HEADS * GLA_DK
GLA_V_W = GLA_HEADS * GLA_DV
IN_WIDTHS = (ATTN_Q_W, ATTN_KV_W, ATTN_KV_W, GLA_QK_W, GLA_QK_W, GLA_V_W, GLA_V_W, 2 * GLA_LOWRANK, 2 * D_MODEL)
LANES = 128
LR_PAD = LANES
V7X_VMEM_CAP = 56 * 1024 * 1024


def _vmem_limit(nbytes):
    return int(min(max(nbytes, 16 * 1024 * 1024), V7X_VMEM_CAP))


def _const_spec(shape):
    nd = len(shape)
    return pl.BlockSpec(shape, lambda *_: (0,) * nd, pipeline_mode=pl.Buffered(1))


def _rms(x):
    return x * lax.rsqrt(jnp.mean(x * x, axis=-1, keepdims=True) + EPS)


def _sigmoid(x):
    return 0.5 * jnp.tanh(0.5 * x) + 0.5


def _log_sigmoid(z):
    return jnp.minimum(z, 0.0) - jnp.log(1.0 + jnp.exp(-jnp.abs(z)))


def _dot(a, b):
    return jnp.dot(a, b, preferred_element_type=F32)


def _dot_nt(a, b):
    return lax.dot_general(a, b, (((1,), (1,)), ((), ())), preferred_element_type=F32)


def _dot_tn(a, b):
    return lax.dot_general(a, b, (((0,), (0,)), ((), ())), preferred_element_type=F32)


def _adaln_kernel(c_ref, w_ref, b_ref, o_ref):
    c = c_ref[...]
    s = (c * _sigmoid(c)).astype(BF16)
    o_ref[...] = _dot(s, w_ref[...].astype(BF16)) + b_ref[...]


def _adaln(cvecs, w_ada, b_ada):
    rows, d = cvecs.shape
    n = w_ada.shape[1]
    bn = n // 4
    return pl.pallas_call(
        _adaln_kernel,
        grid=(n // bn,),
        in_specs=[
            pl.BlockSpec((rows, d), lambda j: (0, 0)),
            pl.BlockSpec((d, bn), lambda j: (0, j)),
            pl.BlockSpec((1, bn), lambda j: (0, j)),
        ],
        out_specs=pl.BlockSpec((rows, bn), lambda j: (0, j)),
        out_shape=jax.ShapeDtypeStruct((rows, n), F32),
        compiler_params=pltpu.CompilerParams(
            dimension_semantics=("arbitrary",),
            vmem_limit_bytes=_vmem_limit(3 * d * bn * 4 + 8 * rows * bn * 4),
        ),
        name="adaln",
    )(cvecs, w_ada, b_ada.reshape(1, n))


def _modulated_norm(x, g, shift, scale):
    return (_rms(x) * g) * (1.0 + scale) + shift


def _head_norm(acc, g, heads, rope):
    outs = []
    for h in range(heads):
        y = _rms(acc[:, h * HEAD_DIM:(h + 1) * HEAD_DIM]) * g
        if rope is not None:
            cosf, sinf = rope
            y = y * cosf + pltpu.roll(y, HEAD_DIM // 2, 1) * sinf
        outs.append(y.astype(BF16))
    return outs


def _log_decays(lr, up_ref, upb_ref):
    z = _dot(lr, up_ref[...]) + upb_ref[...]
    return _log_sigmoid(z) * (1.0 / GLA_GATE_NORM)


def _inproj_latent_kernel(x_ref, sh_ref, sc_ref, g_ref, wqkv_ref, wgla_ref, wlr_ref, wmg_ref, up_ref, upb_ref,
                          qg_ref, kg_ref, cos_ref, sin_ref,
                          q_ref, k_ref, v_ref, gq_ref, gk_ref, gv_ref, sgg_ref, la_ref, smg_ref):
    for r0 in range(0, x_ref.shape[1], INPROJ_ROWS):
        rows = slice(r0, r0 + INPROJ_ROWS)
        hb = _modulated_norm(x_ref[0, rows, :], g_ref[...], sh_ref[0], sc_ref[0]).astype(BF16)
        rope = (cos_ref[rows, :], sin_ref[rows, :])
        lr = _dot(hb, wlr_ref[...]).astype(BF16)
        smg_ref[0, rows, :] = _sigmoid(_dot(hb, wmg_ref[...])).astype(BF16)
        la_ref[0, rows, :] = _log_decays(lr, up_ref, upb_ref)
        gg = _dot(hb, wgla_ref[:, 2 * GLA_QK_W + GLA_V_W:])
        sgg_ref[0, rows, :] = (gg * _sigmoid(gg)).astype(BF16)
        q_acc = _dot(hb, wqkv_ref[:, :ATTN_Q_W])
        for h, y in enumerate(_head_norm(q_acc, qg_ref[...], N_Q_HEADS, rope)):
            q_ref[0, rows, h * HEAD_DIM:(h + 1) * HEAD_DIM] = y
        kv_acc = _dot(hb, wqkv_ref[:, ATTN_Q_W:])
        for h, y in enumerate(_head_norm(kv_acc[:, :ATTN_KV_W], kg_ref[...], N_KV_HEADS, rope)):
            k_ref[0, rows, h * HEAD_DIM:(h + 1) * HEAD_DIM] = y
        v_ref[0, rows, :] = kv_acc[:, ATTN_KV_W:].astype(BF16)
        gqk = _dot(hb, wgla_ref[:, :2 * GLA_QK_W])
        gq_ref[0, rows, :] = gqk[:, :GLA_QK_W].astype(BF16)
        gk_ref[0, rows, :] = gqk[:, GLA_QK_W:].astype(BF16)
        gv_ref[0, rows, :] = _dot(hb, wgla_ref[:, 2 * GLA_QK_W:2 * GLA_QK_W + GLA_V_W]).astype(BF16)


def _inproj_context_kernel(x_ref, sh_ref, sc_ref, g_ref, wkv_ref, wgla_ref, wlr_ref, up_ref, upb_ref, kg_ref,
                           k_ref, v_ref, gk_ref, gv_ref, la_ref):
    hb = _modulated_norm(x_ref[0], g_ref[...], sh_ref[0], sc_ref[0]).astype(BF16)
    lr = _dot(hb, wlr_ref[...]).astype(BF16)
    kv_acc = _dot(hb, wkv_ref[...])
    la_ref[0] = _log_decays(lr, up_ref, upb_ref)
    for h, y in enumerate(_head_norm(kv_acc[:, :ATTN_KV_W], kg_ref[...], N_KV_HEADS, None)):
        k_ref[0, :, h * HEAD_DIM:(h + 1) * HEAD_DIM] = y
    v_ref[0] = kv_acc[:, ATTN_KV_W:].astype(BF16)
    gk_ref[0] = _dot(hb, wgla_ref[:, :GLA_QK_W]).astype(BF16)
    gv_ref[0] = _dot(hb, wgla_ref[:, GLA_QK_W:]).astype(BF16)


def _row_spec(tm, width):
    return pl.BlockSpec((1, tm, width), lambda b, j: (b, j, 0))


def _vec_spec(width):
    return pl.BlockSpec((1, 1, width), lambda b, j: (b, 0, 0))


def _inproj_latent(x, sh, sc, g, wqkv, wgla, wlr, wmg, up, upb, qg, kg, cosf, sinf, tm):
    B, T, D = x.shape
    out_widths = (ATTN_Q_W, ATTN_KV_W, ATTN_KV_W, GLA_QK_W, GLA_QK_W, GLA_V_W, GLA_V_W, 2 * GLA_QK_W, 2 * D_MODEL)
    out_dtypes = (BF16,) * 7 + (F32, BF16)
    weights = (g, wqkv, wgla, wlr, wmg, up, upb, qg, kg)
    w_bytes = sum(int(np.prod(w.shape)) * w.dtype.itemsize for w in weights)
    tile_bytes = tm * D * 4 + sum(tm * w * jnp.dtype(dt).itemsize for w, dt in zip(out_widths, out_dtypes))
    temp_bytes = tm * (D * 6 + 2 * D_MODEL * 4 * 3)
    return pl.pallas_call(
        _inproj_latent_kernel,
        grid=(B, T // tm),
        in_specs=[_row_spec(tm, D), _vec_spec(D), _vec_spec(D)]
        + [_const_spec(w.shape) for w in weights]
        + [pl.BlockSpec((tm, HEAD_DIM), lambda b, j: (j, 0))] * 2,
        out_specs=[_row_spec(tm, w) for w in out_widths],
        out_shape=[jax.ShapeDtypeStruct((B, T, w), dt) for w, dt in zip(out_widths, out_dtypes)],
        compiler_params=pltpu.CompilerParams(
            dimension_semantics=("arbitrary", "arbitrary"),
            vmem_limit_bytes=_vmem_limit(w_bytes + 2 * tile_bytes + temp_bytes),
        ),
        name="inproj_latent",
    )(x, sh, sc, g, wqkv, wgla, wlr, wmg, up, upb, qg, kg, cosf, sinf)


def _inproj_context(ctx, sh, sc, g, wkv, wgla, wlr, up, upb, kg, tm):
    B, Tc, D = ctx.shape
    out_widths = (ATTN_KV_W, ATTN_KV_W, GLA_QK_W, GLA_V_W, 2 * GLA_QK_W)
    out_dtypes = (BF16,) * 4 + (F32,)
    weights = (g, wkv, wgla, wlr, up, upb, kg)
    w_bytes = sum(int(np.prod(w.shape)) * w.dtype.itemsize for w in weights)
    tile_bytes = tm * D * 4 + sum(tm * w * jnp.dtype(dt).itemsize for w, dt in zip(out_widths, out_dtypes))
    temp_bytes = tm * (D * 6 + 2 * D_MODEL * 4 * 3)
    return pl.pallas_call(
        _inproj_context_kernel,
        grid=(B, Tc // tm),
        in_specs=[_row_spec(tm, D), _vec_spec(D), _vec_spec(D)] + [_const_spec(w.shape) for w in weights],
        out_specs=[_row_spec(tm, w) for w in out_widths],
        out_shape=[jax.ShapeDtypeStruct((B, Tc, w), dt) for w, dt in zip(out_widths, out_dtypes)],
        compiler_params=pltpu.CompilerParams(
            dimension_semantics=("arbitrary", "arbitrary"),
            vmem_limit_bytes=_vmem_limit(w_bytes + 2 * tile_bytes + temp_bytes),
        ),
        name="inproj_context",
    )(ctx, sh, sc, g, wkv, wgla, wlr, up, upb, kg)


def _attn_kernel(q_ref, kc_ref, vc_ref, kx_ref, vx_ref, o_ref, k_all, v_aug):
    Tc = kc_ref.shape[1]

    @pl.when(pl.program_id(2) == 0)
    def _():
        k_all[:Tc, :] = kc_ref[0]
        k_all[Tc:, :] = kx_ref[0]
        v_aug[:Tc, :HEAD_DIM] = vc_ref[0]
        v_aug[Tc:, :HEAD_DIM] = vx_ref[0]
        v_aug[:, HEAD_DIM:] = jnp.ones((v_aug.shape[0], HEAD_DIM), BF16)

    chains = [(slice(r * ATTN_ROWS, (r + 1) * ATTN_ROWS), slice(g * HEAD_DIM, (g + 1) * HEAD_DIM))
              for r in range(q_ref.shape[1] // ATTN_ROWS) for g in range(Q_PER_KV)]

    def scores(chain):
        rows, lanes = chain
        return _dot_nt(q_ref[0, rows, lanes], k_all[...])

    s_next = scores(chains[0])
    for i, (rows, lanes) in enumerate(chains):
        s = s_next
        if i + 1 < len(chains):
            s_next = scores(chains[i + 1])
        p = jnp.exp2(s - jnp.max(s, axis=-1, keepdims=True)).astype(BF16)
        oa = _dot(p, v_aug[...])
        o_ref[0, rows, lanes] = (oa[:, :HEAD_DIM] / oa[:, HEAD_DIM:]).astype(BF16)


def _attention(q, k_c, v_c, k_x, v_x, tq):
    B, T, _ = q.shape
    Tc = k_c.shape[1]
    gw = Q_PER_KV * HEAD_DIM
    score_bytes = Q_PER_KV * tq * (T + Tc) * (4 + 2)
    io_bytes = 2 * (2 * tq * gw * 2 + 2 * (T + Tc) * HEAD_DIM * 2) + (T + Tc) * 3 * HEAD_DIM * 2
    return pl.pallas_call(
        _attn_kernel,
        grid=(B, N_KV_HEADS, T // tq),
        in_specs=[
            pl.BlockSpec((1, tq, gw), lambda b, h, j: (b, j, h)),
            pl.BlockSpec((1, Tc, HEAD_DIM), lambda b, h, j: (b, 0, h)),
            pl.BlockSpec((1, Tc, HEAD_DIM), lambda b, h, j: (b, 0, h)),
            pl.BlockSpec((1, T, HEAD_DIM), lambda b, h, j: (b, 0, h)),
            pl.BlockSpec((1, T, HEAD_DIM), lambda b, h, j: (b, 0, h)),
        ],
        out_specs=pl.BlockSpec((1, tq, gw), lambda b, h, j: (b, j, h)),
        out_shape=jax.ShapeDtypeStruct((B, T, ATTN_Q_W), BF16),
        scratch_shapes=[pltpu.VMEM((Tc + T, HEAD_DIM), BF16), pltpu.VMEM((Tc + T, 2 * HEAD_DIM), BF16)],
        compiler_params=pltpu.CompilerParams(
            dimension_semantics=("arbitrary", "arbitrary", "arbitrary"),
            vmem_limit_bytes=_vmem_limit(score_bytes + io_bytes),
        ),
        name="attention",
    )(q, k_c, v_c, k_x, v_x)


def _chunk_scan(x, reverse):
    n, dk = x.shape
    g = min(n, SCAN_ROWS)
    r = lax.broadcasted_iota(jnp.int32, (g, g), 0)
    c = lax.broadcasted_iota(jnp.int32, (g, g), 1)
    same_chunk = (r // GLA_CHUNK) == (c // GLA_CHUNK)
    tri = jnp.where(same_chunk, jnp.where((c >= r) if reverse else (c <= r), 1.0, 0.0), 0.0).astype(BF16)
    hi = x.astype(BF16)
    lo = (x - hi.astype(F32)).astype(BF16)
    parts = jnp.concatenate([hi, lo], axis=1)
    sums = []
    for i in range(n // g):
        y = _dot(tri, parts[i * g:(i + 1) * g, :])
        sums.append(y[:, :dk] + y[:, dk:])
    return jnp.concatenate(sums, axis=0)


def _gla_prep(q, k, la, reverse, qe_ref, ke_ref, tot_ref, first_row):
    b = _chunk_scan(la, reverse)
    ke_ref[...] = (k.astype(F32) * jnp.exp(-b)).astype(BF16)
    if q is not None:
        qe_ref[...] = (q.astype(F32) * jnp.exp(b) * (GLA_DK ** -0.5)).astype(BF16)
    last = 0 if reverse else GLA_CHUNK - 1
    for n in range(la.shape[0] // GLA_CHUNK):
        row = n * GLA_CHUNK + last
        tot_ref[first_row + n:first_row + n + 1, :] = b[row:row + 1, :]


def _gla_kernel(kc_ref, vc_ref, lafc_ref, labc_ref, q_ref, k_ref, v_ref, laf_ref, lab_ref, sgg_ref, gn_ref,
                o_ref, acc_ref, qe_ref, ke_ref, kec_ref, tot_ref, u_ref):
    C = GLA_CHUNK
    n_ctx = kc_ref.shape[1] // C
    n_lat = q_ref.shape[1] // C

    tot_ref[...] = jnp.zeros_like(tot_ref)
    for d, (lac, la) in enumerate(((lafc_ref, laf_ref), (labc_ref, lab_ref))):
        _gla_prep(q_ref[0], k_ref[0], la[0], d == 1, qe_ref.at[d], ke_ref.at[d], tot_ref.at[d], 0)
        _gla_prep(None, kc_ref[0], lac[0], d == 1, None, kec_ref.at[d], tot_ref.at[d], n_lat)
    decay = [jnp.exp(tot_ref[d].T) for d in range(2)]

    def rows(i):
        return slice(i * C, (i + 1) * C)

    r_id = lax.broadcasted_iota(jnp.int32, (C, C), 0)
    c_id = lax.broadcasted_iota(jnp.int32, (C, C), 1)
    scores = []
    for n in range(n_lat):
        r = rows(n)
        a = (jnp.where(c_id <= r_id, _dot_nt(qe_ref[0, r, :], ke_ref[0, r, :]), 0.0)
             + jnp.where(c_id >= r_id, _dot_nt(qe_ref[1, r, :], ke_ref[1, r, :]), 0.0))
        scores.append(a.astype(BF16))

    for d in range(2):
        for n in range(n_lat):
            u_ref[d, n] = _dot_tn(ke_ref[d, rows(n), :], v_ref[0, rows(n), :])
        for n in range(n_ctx):
            u_ref[d, n_lat + n] = _dot_tn(kec_ref[d, rows(n), :], vc_ref[0, rows(n), :])

    for n in range(n_lat):
        acc_ref[rows(n), :] = _dot(scores[n], v_ref[0, rows(n), :])

    def step(state, d, n):
        col = decay[d][:, n:n + 1]
        return col * state + col * u_ref[d, n]

    s_fwd = jnp.zeros((GLA_DK, GLA_DV), F32)
    s_bwd = jnp.zeros((GLA_DK, GLA_DV), F32)
    for i in range(n_ctx):
        s_fwd = step(s_fwd, 0, n_lat + i)
        s_bwd = step(s_bwd, 1, n_lat + n_ctx - 1 - i)

    for i in range(n_lat):
        j = n_lat - 1 - i
        acc_ref[rows(i), :] += _dot(qe_ref[0, rows(i), :], s_fwd.astype(BF16))
        s_fwd = step(s_fwd, 0, i)
        acc_ref[rows(j), :] += _dot(qe_ref[1, rows(j), :], s_bwd.astype(BF16))
        s_bwd = step(s_bwd, 1, j)

    go = _rms(acc_ref[...]) * gn_ref[...] * sgg_ref[0].astype(F32)
    o_ref[0] = go.astype(BF16)


def _gla(gk_c, gv_c, la_c, gq, gk, gv, la, sgg, gn):
    B, T, _ = gq.shape
    Tc = gk_c.shape[1]
    H = GLA_HEADS
    n_lat, n_ctx = T // GLA_CHUNK, Tc // GLA_CHUNK

    def blk(t, w, off=0):
        return pl.BlockSpec((1, t, w), lambda b, h: (b, 0, h + off))

    io_bytes = 2 * (Tc * (GLA_DK * 2 + GLA_DV * 2 + 2 * GLA_DK * 4)
                    + T * (2 * GLA_DK * 2 + GLA_DV * 2 + 2 * GLA_DK * 4 + 2 * GLA_DV * 2))
    scratch = [
        pltpu.VMEM((T, GLA_DV), F32),
        pltpu.VMEM((2, T, GLA_DK), BF16),
        pltpu.VMEM((2, T, GLA_DK), BF16),
        pltpu.VMEM((2, Tc, GLA_DK), BF16),
        pltpu.VMEM((2, GLA_DK, GLA_DK), F32),
        pltpu.VMEM((2, n_lat + n_ctx, GLA_DK, GLA_DV), F32),
    ]
    assert n_lat + n_ctx <= GLA_DK
    scratch_bytes = (T * GLA_DV * 4 + 4 * (T + Tc) * GLA_DK * 2 + 2 * GLA_DK * GLA_DK * 4
                     + 2 * (n_lat + n_ctx) * GLA_DK * GLA_DV * 4)
    temp_bytes = 6 * T * GLA_DK * 4 + 3 * T * GLA_DV * 4
    return pl.pallas_call(
        _gla_kernel,
        grid=(B, H),
        in_specs=[
            blk(Tc, GLA_DK), blk(Tc, GLA_DV), blk(Tc, GLA_DK), blk(Tc, GLA_DK, H),
            blk(T, GLA_DK), blk(T, GLA_DK), blk(T, GLA_DV), blk(T, GLA_DK), blk(T, GLA_DK, H),
            blk(T, GLA_DV),
            pl.BlockSpec((1, GLA_DV), lambda b, h: (0, 0)),
        ],
        out_specs=blk(T, GLA_DV),
        out_shape=jax.ShapeDtypeStruct((B, T, GLA_V_W), BF16),
        scratch_shapes=scratch,
        compiler_params=pltpu.CompilerParams(
            dimension_semantics=("arbitrary", "arbitrary"),
            vmem_limit_bytes=_vmem_limit(io_bytes + scratch_bytes + temp_bytes),
        ),
        name="gla",
    )(gk_c, gv_c, la_c, la_c, gq, gk, gv, la, la, sgg, gn)


def _merge_ffn_kernel(attn_ref, go_ref, smg_ref, x_ref, g1_ref, sh2_ref, sc2_ref, g2_ref,
                      n2_ref, wa_ref, wg_ref, wo_ref, w1_ref, w2_ref, o_ref):
    blocks = [slice(r0, r0 + MERGE_ROWS) for r0 in range(0, x_ref.shape[1], MERGE_ROWS)]
    branch = [(_dot(attn_ref[0, r, :], wa_ref[...]), _dot(go_ref[0, r, :], wg_ref[...])) for r in blocks]
    x1 = []
    for r, (ya, yg) in zip(blocks, branch):
        smg = smg_ref[0, r, :]
        merged = smg[:, :D_MODEL].astype(F32) * ya + smg[:, D_MODEL:].astype(F32) * yg
        x1.append(x_ref[0, r, :] + g1_ref[0] * _dot(merged.astype(BF16), wo_ref[...]))
    hidden = []
    for xr in x1:
        h2 = _modulated_norm(xr, n2_ref[...], sh2_ref[0], sc2_ref[0]).astype(BF16)
        hidden.append((_dot(h2, w1_ref[:, :D_FF]), _dot(h2, w1_ref[:, D_FF:])))
    for r, xr, (a, b) in zip(blocks, x1, hidden):
        u = (a * _sigmoid(a) * b).astype(BF16)
        o_ref[0, r, :] = xr + g2_ref[0] * _dot(u, w2_ref[...])


def _merge_ffn(attn, go, smg, x, g1, sh2, sc2, g2, n2, wa, wg, wo, w1, w2, tm):
    B, T, D = x.shape
    weights = (n2, wa, wg, wo, w1, w2)
    w_bytes = sum(int(np.prod(w.shape)) * w.dtype.itemsize for w in weights)
    tile_bytes = tm * (D * 2 + D * 2 + 2 * D * 2 + D * 4 + D * 4)
    temp_bytes = tm * (2 * D_FF * 4 + D_FF * 2 + 6 * D * 4)
    return pl.pallas_call(
        _merge_ffn_kernel,
        grid=(B, T // tm),
        in_specs=[_row_spec(tm, D), _row_spec(tm, D), _row_spec(tm, 2 * D), _row_spec(tm, D)]
        + [_vec_spec(D)] * 4 + [_const_spec(w.shape) for w in weights],
        out_specs=_row_spec(tm, D),
        out_shape=jax.ShapeDtypeStruct((B, T, D), F32),
        compiler_params=pltpu.CompilerParams(
            dimension_semantics=("arbitrary", "arbitrary"),
            vmem_limit_bytes=_vmem_limit(w_bytes + 2 * tile_bytes + temp_bytes),
        ),
        name="merge_ffn",
    )(attn, go, smg, x, g1, sh2, sc2, g2, n2, wa, wg, wo, w1, w2)


def _deinterleave_perm(heads):
    one = np.concatenate([np.arange(0, HEAD_DIM, 2), np.arange(1, HEAD_DIM, 2)])
    return np.concatenate([h * HEAD_DIM + one for h in range(heads)])


def _rope_tables(T):
    rows = T // GRID_W
    row = jnp.repeat(jnp.arange(rows, dtype=F32), GRID_W)
    col = jnp.tile(jnp.arange(GRID_W, dtype=F32), rows)
    half = HEAD_DIM // 2
    inv_freq = 1.0 / (ROPE_THETA ** (jnp.arange(0, half, 2, dtype=F32) / half))
    ang = jnp.concatenate([row[:, None] * inv_freq[None], col[:, None] * inv_freq[None]], axis=-1)
    cos, sin = jnp.cos(ang), jnp.sin(ang)
    return jnp.concatenate([cos, cos], axis=-1), jnp.concatenate([-sin, sin], axis=-1)


def _block(x, c, ctx, c_ctx, w_ada, b_ada, norm1_g, w_in, q_norm_g, k_norm_g, gk_up_f, gk_up_f_b, gk_up_b, gk_up_b_b,
           gla_norm_g, w_attn_proj, w_gla_proj, w_out, norm2_g, w_ffn_in, w_ffn_out):
    B, T, D = x.shape
    Tc = ctx.shape[1]

    mod = _adaln(jnp.concatenate([c, c_ctx[None, :]], axis=0), w_ada, b_ada)
    sh1, sc1, g1, sh2, sc2, g2 = [m[:B, None, :] for m in jnp.split(mod, 6, axis=-1)]
    sh1c, sc1c = [jnp.broadcast_to(m[B:, None, :], (B, 1, D)) for m in jnp.split(mod, 6, axis=-1)[:2]]

    offs = np.concatenate([[0], np.cumsum(IN_WIDTHS)])
    cols = [w_in[:, offs[i]:offs[i + 1]] for i in range(len(IN_WIDTHS))]
    w_q = cols[0][:, _deinterleave_perm(N_Q_HEADS)]
    w_k = cols[1][:, _deinterleave_perm(N_KV_HEADS)]
    wqkv = jnp.concatenate([w_q, w_k, cols[2]], axis=1).astype(BF16)
    wkv = wqkv[:, ATTN_Q_W:]
    wgla = jnp.concatenate(cols[3:7], axis=1).astype(BF16)
    wgla_c = wgla[:, GLA_QK_W:2 * GLA_QK_W + GLA_V_W]
    wlr = jnp.pad(cols[7], ((0, 0), (0, LR_PAD - 2 * GLA_LOWRANK))).astype(BF16)
    wmg = cols[8].astype(BF16)
    up = jnp.zeros((LR_PAD, 2 * GLA_QK_W), F32)
    up = up.at[:GLA_LOWRANK, :GLA_QK_W].set(gk_up_f).at[GLA_LOWRANK:2 * GLA_LOWRANK, GLA_QK_W:].set(gk_up_b).astype(BF16)
    upb = jnp.concatenate([gk_up_f_b, gk_up_b_b])[None, :]
    perm1 = _deinterleave_perm(1)
    qg = (q_norm_g[perm1] * (HEAD_DIM ** -0.5 * LOG2_E))[None, :]
    kg = k_norm_g[perm1][None, :]
    n1 = norm1_g[None, :]
    cosf, sinf = _rope_tables(T)

    k_c, v_c, gk_c, gv_c, la_c = _inproj_context(ctx, sh1c, sc1c, n1, wkv, wgla_c, wlr, up, upb, kg, tm=min(Tc, 256))
    q, k_x, v_x, gq, gk, gv, sgg, la, smg = _inproj_latent(
        x, sh1, sc1, n1, wqkv, wgla, wlr, wmg, up, upb, qg, kg, cosf, sinf, tm=min(T, 512))

    attn = _attention(q, k_c, v_c, k_x, v_x, tq=min(T, 1024))
    go = _gla(gk_c, gv_c, la_c, gq, gk, gv, la, sgg, gla_norm_g[None, :])
    return _merge_ffn(attn, go, smg, x, g1, sh2, sc2, g2, norm2_g[None, :], w_attn_proj.astype(BF16),
                      w_gla_proj.astype(BF16), w_out.astype(BF16), w_ffn_in.astype(BF16), w_ffn_out.astype(BF16),
                      tm=min(T, 512))


def kernel(x, c, ctx, c_ctx, w_ada, b_ada, norm1_g, w_in, q_norm_g, k_norm_g, gk_up_f, gk_up_f_b, gk_up_b, gk_up_b_b,
           gla_norm_g, w_attn_proj, w_gla_proj, w_out, norm2_g, w_ffn_in, w_ffn_out):
    assert w_ada.shape[0] == 1, "single trunk layer"
    return _block(x, c, ctx, c_ctx, w_ada[0], b_ada[0], norm1_g[0], w_in[0], q_norm_g[0], k_norm_g[0], gk_up_f[0],
                  gk_up_f_b[0], gk_up_b[0], gk_up_b_b[0], gla_norm_g[0], w_attn_proj[0], w_gla_proj[0], w_out[0],
                  norm2_g[0], w_ffn_in[0], w_ffn_out[0])
```

```python
import functools

import numpy as np
import jax
import jax.numpy as jnp
from jax import lax
from jax.experimental import pallas as pl
from jax.experimental.pallas import tpu as pltpu

F32 = jnp.float32
BF16 = jnp.bfloat16

D_MODEL = 1024
GRID_W = 64
HEAD_DIM = 128
N_Q_HEADS = 8
N_KV_HEADS = 2
Q_PER_KV = N_Q_HEADS // N_KV_HEADS
ROPE_THETA = 10000.0
GLA_HEADS = 4
GLA_DK = 128
GLA_DV = 256
GLA_LOWRANK = 16
GLA_GATE_NORM = 16.0
GLA_CHUNK = 64
D_FF = 2816
EPS = 1e-6
LOG2_E = 1.4426950408889634
MERGE_ROWS = 256
INPROJ_ROWS = 256
SCAN_ROWS = 256
ATTN_ROWS = 256
GLA_UNROLL = 2

ATTN_Q_W = N_Q_HEADS * HEAD_DIM
ATTN_KV_W = N_KV_HEADS * HEAD_DIM
GLA_QK_W = GLA_HEADS * GLA_DK
GLA_V_W = GLA_HEADS * GLA_DV
IN_WIDTHS = (ATTN_Q_W, ATTN_KV_W, ATTN_KV_W, GLA_QK_W, GLA_QK_W, GLA_V_W, GLA_V_W, 2 * GLA_LOWRANK, 2 * D_MODEL)
LANES = 128
LR_PAD = LANES
V7X_VMEM_CAP = 56 * 1024 * 1024


def _vmem_limit(nbytes):
    return int(min(max(nbytes, 16 * 1024 * 1024), V7X_VMEM_CAP))


def _const_spec(shape):
    nd = len(shape)
    return pl.BlockSpec(shape, lambda *_: (0,) * nd, pipeline_mode=pl.Buffered(1))


def _rms(x):
    return x * lax.rsqrt(jnp.mean(x * x, axis=-1, keepdims=True) + EPS)


def _sigmoid(x):
    return 1.0 / (1.0 + jnp.exp(-x))


def _log_sigmoid(z):
    return jnp.minimum(z, 0.0) - jnp.log(1.0 + jnp.exp(-jnp.abs(z)))


def _dot(a, b):
    return jnp.dot(a, b, preferred_element_type=F32)


def _dot_nt(a, b):
    return lax.dot_general(a, b, (((1,), (1,)), ((), ())), preferred_element_type=F32)


def _dot_tn(a, b):
    return lax.dot_general(a, b, (((0,), (0,)), ((), ())), preferred_element_type=F32)


def _adaln_kernel(c_ref, w_ref, b_ref, o_ref):
    c = c_ref[...]
    s = (c * _sigmoid(c)).astype(BF16)
    o_ref[...] = _dot(s, w_ref[...].astype(BF16)) + b_ref[...]


def _adaln(cvecs, w_ada, b_ada):
    rows, d = cvecs.shape
    n = w_ada.shape[1]
    bn = n // 4
    return pl.pallas_call(
        _adaln_kernel,
        grid=(n // bn,),
        in_specs=[
            pl.BlockSpec((rows, d), lambda j: (0, 0)),
            pl.BlockSpec((d, bn), lambda j: (0, j)),
            pl.BlockSpec((1, bn), lambda j: (0, j)),
        ],
        out_specs=pl.BlockSpec((rows, bn), lambda j: (0, j)),
        out_shape=jax.ShapeDtypeStruct((rows, n), F32),
        compiler_params=pltpu.CompilerParams(
            dimension_semantics=("arbitrary",),
            vmem_limit_bytes=_vmem_limit(3 * d * bn * 4 + 8 * rows * bn * 4),
        ),
        name="adaln",
    )(cvecs, w_ada, b_ada.reshape(1, n))


def _modulated_norm(x, g, shift, scale):
    return (_rms(x) * g) * (1.0 + scale) + shift


def _head_norm(acc, g, heads, rope):
    outs = []
    for h in range(heads):
        y = _rms(acc[:, h * HEAD_DIM:(h + 1) * HEAD_DIM]) * g
        if rope is not None:
            cosf, sinf = rope
            y = y * cosf + pltpu.roll(y, HEAD_DIM // 2, 1) * sinf
        outs.append(y.astype(BF16))
    return outs


def _log_decays(lr, up_ref, upb_ref):
    z = _dot(lr, up_ref[...]) + upb_ref[...]
    return _log_sigmoid(z) * (1.0 / GLA_GATE_NORM)


def _inproj_latent_kernel(x_ref, sh_ref, sc_ref, g_ref, wqkv_ref, wgla_ref, wlr_ref, wmg_ref, up_ref, upb_ref,
                          qg_ref, kg_ref, cos_ref, sin_ref,
                          q_ref, k_ref, v_ref, gq_ref, gk_ref, gv_ref, sgg_ref, la_ref, smg_ref):
    for r0 in range(0, x_ref.shape[1], INPROJ_ROWS):
        rows = slice(r0, r0 + INPROJ_ROWS)
        hb = _modulated_norm(x_ref[0, rows, :], g_ref[...], sh_ref[0], sc_ref[0]).astype(BF16)
        rope = (cos_ref[rows, :], sin_ref[rows, :])
        lr = _dot(hb, wlr_ref[...]).astype(BF16)
        smg_ref[0, rows, :] = _sigmoid(_dot(hb, wmg_ref[...])).astype(BF16)
        la_ref[0, rows, :] = _log_decays(lr, up_ref, upb_ref)
        gg = _dot(hb, wgla_ref[:, 2 * GLA_QK_W + GLA_V_W:])
        sgg_ref[0, rows, :] = (gg * _sigmoid(gg)).astype(BF16)
        q_acc = _dot(hb, wqkv_ref[:, :ATTN_Q_W])
        for h, y in enumerate(_head_norm(q_acc, qg_ref[...], N_Q_HEADS, rope)):
            q_ref[0, rows, h * HEAD_DIM:(h + 1) * HEAD_DIM] = y
        kv_acc = _dot(hb, wqkv_ref[:, ATTN_Q_W:])
        for h, y in enumerate(_head_norm(kv_acc[:, :ATTN_KV_W], kg_ref[...], N_KV_HEADS, rope)):
            k_ref[0, rows, h * HEAD_DIM:(h + 1) * HEAD_DIM] = y
        v_ref[0, rows, :] = kv_acc[:, ATTN_KV_W:].astype(BF16)
        gqk = _dot(hb, wgla_ref[:, :2 * GLA_QK_W])
        gq_ref[0, rows, :] = gqk[:, :GLA_QK_W].astype(BF16)
        gk_ref[0, rows, :] = gqk[:, GLA_QK_W:].astype(BF16)
        gv_ref[0, rows, :] = _dot(hb, wgla_ref[:, 2 * GLA_QK_W:2 * GLA_QK_W + GLA_V_W]).astype(BF16)


def _inproj_context_kernel(x_ref, sh_ref, sc_ref, g_ref, wkv_ref, wgla_ref, wlr_ref, up_ref, upb_ref, kg_ref,
                           k_ref, v_ref, gk_ref, gv_ref, la_ref):
    hb = _modulated_norm(x_ref[0], g_ref[...], sh_ref[0], sc_ref[0]).astype(BF16)
    lr = _dot(hb, wlr_ref[...]).astype(BF16)
    kv_acc = _dot(hb, wkv_ref[:, ATTN_Q_W:])
    la_ref[0] = _log_decays(lr, up_ref, upb_ref)
    for h, y in enumerate(_head_norm(kv_acc[:, :ATTN_KV_W], kg_ref[...], N_KV_HEADS, None)):
        k_ref[0, :, h * HEAD_DIM:(h + 1) * HEAD_DIM] = y
    v_ref[0] = kv_acc[:, ATTN_KV_W:].astype(BF16)
    gk_ref[0] = _dot(hb, wgla_ref[:, GLA_QK_W:2 * GLA_QK_W]).astype(BF16)
    gv_ref[0] = _dot(hb, wgla_ref[:, 2 * GLA_QK_W:2 * GLA_QK_W + GLA_V_W]).astype(BF16)


def _row_spec(tm, width):
    return pl.BlockSpec((1, tm, width), lambda b, j: (b, j, 0))


def _vec_spec(width):
    return pl.BlockSpec((1, 1, width), lambda b, j: (b, 0, 0))


def _inproj_latent(x, sh, sc, g, wqkv, wgla, wlr, wmg, up, upb, qg, kg, cosf, sinf, tm):
    B, T, D = x.shape
    out_widths = (ATTN_Q_W, ATTN_KV_W, ATTN_KV_W, GLA_QK_W, GLA_QK_W, GLA_V_W, GLA_V_W, 2 * GLA_QK_W, 2 * D_MODEL)
    out_dtypes = (BF16,) * 7 + (F32, BF16)
    weights = (g, wqkv, wgla, wlr, wmg, up, upb, qg, kg)
    w_bytes = sum(int(np.prod(w.shape)) * w.dtype.itemsize for w in weights)
    tile_bytes = tm * D * 4 + sum(tm * w * jnp.dtype(dt).itemsize for w, dt in zip(out_widths, out_dtypes))
    temp_bytes = tm * (D * 6 + 2 * D_MODEL * 4 * 3)
    return pl.pallas_call(
        _inproj_latent_kernel,
        grid=(B, T // tm),
        in_specs=[_row_spec(tm, D), _vec_spec(D), _vec_spec(D)]
        + [_const_spec(w.shape) for w in weights]
        + [pl.BlockSpec((tm, HEAD_DIM), lambda b, j: (j, 0))] * 2,
        out_specs=[_row_spec(tm, w) for w in out_widths],
        out_shape=[jax.ShapeDtypeStruct((B, T, w), dt) for w, dt in zip(out_widths, out_dtypes)],
        compiler_params=pltpu.CompilerParams(
            dimension_semantics=("arbitrary", "arbitrary"),
            vmem_limit_bytes=_vmem_limit(w_bytes + 2 * tile_bytes + temp_bytes),
        ),
        name="inproj_latent",
    )(x, sh, sc, g, wqkv, wgla, wlr, wmg, up, upb, qg, kg, cosf, sinf)


def _inproj_context(ctx, sh, sc, g, wkv, wgla, wlr, up, upb, kg, tm):
    B, Tc, D = ctx.shape
    out_widths = (ATTN_KV_W, ATTN_KV_W, GLA_QK_W, GLA_V_W, 2 * GLA_QK_W)
    out_dtypes = (BF16,) * 4 + (F32,)
    weights = (g, wkv, wgla, wlr, up, upb, kg)
    w_bytes = sum(int(np.prod(w.shape)) * w.dtype.itemsize for w in weights)
    tile_bytes = tm * D * 4 + sum(tm * w * jnp.dtype(dt).itemsize for w, dt in zip(out_widths, out_dtypes))
    temp_bytes = tm * (D * 6 + 2 * D_MODEL * 4 * 3)
    return pl.pallas_call(
        _inproj_context_kernel,
        grid=(B, Tc // tm),
        in_specs=[_row_spec(tm, D), _vec_spec(D), _vec_spec(D)] + [_const_spec(w.shape) for w in weights],
        out_specs=[_row_spec(tm, w) for w in out_widths],
        out_shape=[jax.ShapeDtypeStruct((B, Tc, w), dt) for w, dt in zip(out_widths, out_dtypes)],
        compiler_params=pltpu.CompilerParams(
            dimension_semantics=("arbitrary", "arbitrary"),
            vmem_limit_bytes=_vmem_limit(w_bytes + 2 * tile_bytes + temp_bytes),
        ),
        name="inproj_context",
    )(ctx, sh, sc, g, wkv, wgla, wlr, up, upb, kg)


def _attn_kernel(q_ref, kc_ref, vc_ref, kx_ref, vx_ref, o_ref, k_all, v_aug):
    Tc = kc_ref.shape[1]

    @pl.when(pl.program_id(2) == 0)
    def _():
        k_all[:Tc, :] = kc_ref[0]
        k_all[Tc:, :] = kx_ref[0]
        v_aug[:Tc, :HEAD_DIM] = vc_ref[0]
        v_aug[Tc:, :HEAD_DIM] = vx_ref[0]
        v_aug[:, HEAD_DIM:] = jnp.ones((v_aug.shape[0], HEAD_DIM), BF16)

    chains = [(slice(r * ATTN_ROWS, (r + 1) * ATTN_ROWS), slice(g * HEAD_DIM, (g + 1) * HEAD_DIM))
              for r in range(q_ref.shape[1] // ATTN_ROWS) for g in range(Q_PER_KV)]

    def scores(chain):
        rows, lanes = chain
        return _dot_nt(q_ref[0, rows, lanes], k_all[...])

    s_next = scores(chains[0])
    for i, (rows, lanes) in enumerate(chains):
        s = s_next
        if i + 1 < len(chains):
            s_next = scores(chains[i + 1])
        p = jnp.exp2(s - jnp.max(s, axis=-1, keepdims=True)).astype(BF16)
        oa = _dot(p, v_aug[...])
        o_ref[0, rows, lanes] = (oa[:, :HEAD_DIM] / oa[:, HEAD_DIM:]).astype(BF16)


def _attention(q, k_c, v_c, k_x, v_x, tq):
    B, T, _ = q.shape
    Tc = k_c.shape[1]
    gw = Q_PER_KV * HEAD_DIM
    score_bytes = Q_PER_KV * tq * (T + Tc) * (4 + 2)
    io_bytes = 2 * (2 * tq * gw * 2 + 2 * (T + Tc) * HEAD_DIM * 2) + (T + Tc) * 3 * HEAD_DIM * 2
    return pl.pallas_call(
        _attn_kernel,
        grid=(B, N_KV_HEADS, T // tq),
        in_specs=[
            pl.BlockSpec((1, tq, gw), lambda b, h, j: (b, j, h)),
            pl.BlockSpec((1, Tc, HEAD_DIM), lambda b, h, j: (b, 0, h)),
            pl.BlockSpec((1, Tc, HEAD_DIM), lambda b, h, j: (b, 0, h)),
            pl.BlockSpec((1, T, HEAD_DIM), lambda b, h, j: (b, 0, h)),
            pl.BlockSpec((1, T, HEAD_DIM), lambda b, h, j: (b, 0, h)),
        ],
        out_specs=pl.BlockSpec((1, tq, gw), lambda b, h, j: (b, j, h)),
        out_shape=jax.ShapeDtypeStruct((B, T, ATTN_Q_W), BF16),
        scratch_shapes=[pltpu.VMEM((Tc + T, HEAD_DIM), BF16), pltpu.VMEM((Tc + T, 2 * HEAD_DIM), BF16)],
        compiler_params=pltpu.CompilerParams(
            dimension_semantics=("arbitrary", "arbitrary", "arbitrary"),
            vmem_limit_bytes=_vmem_limit(score_bytes + io_bytes),
        ),
        name="attention",
    )(q, k_c, v_c, k_x, v_x)


def _chunk_scan(x, reverse):
    n, dk = x.shape
    g = min(n, SCAN_ROWS)
    r = lax.broadcasted_iota(jnp.int32, (g, g), 0)
    c = lax.broadcasted_iota(jnp.int32, (g, g), 1)
    same_chunk = (r // GLA_CHUNK) == (c // GLA_CHUNK)
    tri = jnp.where(same_chunk, jnp.where((c >= r) if reverse else (c <= r), 1.0, 0.0), 0.0).astype(BF16)
    hi = x.astype(BF16)
    lo = (x - hi.astype(F32)).astype(BF16)
    parts = jnp.concatenate([hi, lo], axis=1)
    sums = []
    for i in range(n // g):
        y = _dot(tri, parts[i * g:(i + 1) * g, :])
        sums.append(y[:, :dk] + y[:, dk:])
    return jnp.concatenate(sums, axis=0)


def _gla_prep(q, k, la, reverse, qe_ref, ke_ref, tot_ref, first_row):
    b = _chunk_scan(la, reverse)
    ke_ref[...] = (k.astype(F32) * jnp.exp(-b)).astype(BF16)
    if q is not None:
        qe_ref[...] = (q.astype(F32) * jnp.exp(b) * (GLA_DK ** -0.5)).astype(BF16)
    last = 0 if reverse else GLA_CHUNK - 1
    for n in range(la.shape[0] // GLA_CHUNK):
        row = n * GLA_CHUNK + last
        tot_ref[first_row + n:first_row + n + 1, :] = b[row:row + 1, :]


def _gla_kernel(kc_ref, vc_ref, lafc_ref, labc_ref, q_ref, k_ref, v_ref, laf_ref, lab_ref, sgg_ref, gn_ref,
                o_ref, acc_ref, qe_ref, ke_ref, kec_ref, tot_ref, u_ref):
    C = GLA_CHUNK
    n_ctx = kc_ref.shape[1] // C
    n_lat = q_ref.shape[1] // C

    tot_ref[...] = jnp.zeros_like(tot_ref)
    for d, (lac, la) in enumerate(((lafc_ref, laf_ref), (labc_ref, lab_ref))):
        _gla_prep(q_ref[0], k_ref[0], la[0], d == 1, qe_ref.at[d], ke_ref.at[d], tot_ref.at[d], 0)
        _gla_prep(None, kc_ref[0], lac[0], d == 1, None, kec_ref.at[d], tot_ref.at[d], n_lat)
    decay = [jnp.exp(tot_ref[d].T) for d in range(2)]

    def rows(i):
        return slice(i * C, (i + 1) * C)

    r_id = lax.broadcasted_iota(jnp.int32, (C, C), 0)
    c_id = lax.broadcasted_iota(jnp.int32, (C, C), 1)
    scores = []
    for n in range(n_lat):
        r = rows(n)
        a = (jnp.where(c_id <= r_id, _dot_nt(qe_ref[0, r, :], ke_ref[0, r, :]), 0.0)
             + jnp.where(c_id >= r_id, _dot_nt(qe_ref[1, r, :], ke_ref[1, r, :]), 0.0))
        scores.append(a.astype(BF16))

    for d in range(2):
        for n in range(n_lat):
            u_ref[d, n] = _dot_tn(ke_ref[d, rows(n), :], v_ref[0, rows(n), :])
        for n in range(n_ctx):
            u_ref[d, n_lat + n] = _dot_tn(kec_ref[d, rows(n), :], vc_ref[0, rows(n), :])

    for n in range(n_lat):
        acc_ref[rows(n), :] = _dot(scores[n], v_ref[0, rows(n), :])

    def step(state, d, n):
        col = decay[d][:, n:n + 1]
        return col * state + col * u_ref[d, n]

    s_fwd = jnp.zeros((GLA_DK, GLA_DV), F32)
    s_bwd = jnp.zeros((GLA_DK, GLA_DV), F32)
    for i in range(n_ctx):
        s_fwd = step(s_fwd, 0, n_lat + i)
        s_bwd = step(s_bwd, 1, n_lat + n_ctx - 1 - i)

    for i in range(n_lat):
        j = n_lat - 1 - i
        acc_ref[rows(i), :] += _dot(qe_ref[0, rows(i), :], s_fwd.astype(BF16))
        s_fwd = step(s_fwd, 0, i)
        acc_ref[rows(j), :] += _dot(qe_ref[1, rows(j), :], s_bwd.astype(BF16))
        s_bwd = step(s_bwd, 1, j)

    go = _rms(acc_ref[...]) * gn_ref[...] * sgg_ref[0].astype(F32)
    o_ref[0] = go.astype(BF16)


def _gla(gk_c, gv_c, la_c, gq, gk, gv, la, sgg, gn):
    B, T, _ = gq.shape
    Tc = gk_c.shape[1]
    H = GLA_HEADS
    n_lat, n_ctx = T // GLA_CHUNK, Tc // GLA_CHUNK

    def blk(t, w, off=0):
        return pl.BlockSpec((1, t, w), lambda b, h: (b, 0, h + off))

    io_bytes = 2 * (Tc * (GLA_DK * 2 + GLA_DV * 2 + 2 * GLA_DK * 4)
                    + T * (2 * GLA_DK * 2 + GLA_DV * 2 + 2 * GLA_DK * 4 + 2 * GLA_DV * 2))
    scratch = [
        pltpu.VMEM((T, GLA_DV), F32),
        pltpu.VMEM((2, T, GLA_DK), BF16),
        pltpu.VMEM((2, T, GLA_DK), BF16),
        pltpu.VMEM((2, Tc, GLA_DK), BF16),
        pltpu.VMEM((2, GLA_DK, GLA_DK), F32),
        pltpu.VMEM((2, n_lat + n_ctx, GLA_DK, GLA_DV), F32),
    ]
    assert n_lat + n_ctx <= GLA_DK
    scratch_bytes = (T * GLA_DV * 4 + 4 * (T + Tc) * GLA_DK * 2 + 2 * GLA_DK * GLA_DK * 4
                     + 2 * (n_lat + n_ctx) * GLA_DK * GLA_DV * 4)
    temp_bytes = 6 * T * GLA_DK * 4 + 3 * T * GLA_DV * 4
    return pl.pallas_call(
        _gla_kernel,
        grid=(B, H),
        in_specs=[
            blk(Tc, GLA_DK), blk(Tc, GLA_DV), blk(Tc, GLA_DK), blk(Tc, GLA_DK, H),
            blk(T, GLA_DK), blk(T, GLA_DK), blk(T, GLA_DV), blk(T, GLA_DK), blk(T, GLA_DK, H),
            blk(T, GLA_DV),
            pl.BlockSpec((1, GLA_DV), lambda b, h: (0, 0)),
        ],
        out_specs=blk(T, GLA_DV),
        out_shape=jax.ShapeDtypeStruct((B, T, GLA_V_W), BF16),
        scratch_shapes=scratch,
        compiler_params=pltpu.CompilerParams(
            dimension_semantics=("arbitrary", "arbitrary"),
            vmem_limit_bytes=_vmem_limit(io_bytes + scratch_bytes + temp_bytes),
        ),
        name="gla",
    )(gk_c, gv_c, la_c, la_c, gq, gk, gv, la, la, sgg, gn)


def _merge_ffn_kernel(attn_ref, go_ref, smg_ref, x_ref, g1_ref, sh2_ref, sc2_ref, g2_ref,
                      n2_ref, wa_ref, wg_ref, wo_ref, w1_ref, w2_ref, o_ref):
    blocks = [slice(r0, r0 + MERGE_ROWS) for r0 in range(0, x_ref.shape[1], MERGE_ROWS)]
    branch = [(_dot(attn_ref[0, r, :], wa_ref[...]), _dot(go_ref[0, r, :], wg_ref[...])) for r in blocks]
    x1 = []
    for r, (ya, yg) in zip(blocks, branch):
        smg = smg_ref[0, r, :]
        merged = smg[:, :D_MODEL].astype(F32) * ya + smg[:, D_MODEL:].astype(F32) * yg
        x1.append(x_ref[0, r, :] + g1_ref[0] * _dot(merged.astype(BF16), wo_ref[...]))
    hidden = []
    for xr in x1:
        h2 = _modulated_norm(xr, n2_ref[...], sh2_ref[0], sc2_ref[0]).astype(BF16)
        hidden.append((_dot(h2, w1_ref[:, :D_FF]), _dot(h2, w1_ref[:, D_FF:])))
    for r, xr, (a, b) in zip(blocks, x1, hidden):
        u = (a * _sigmoid(a) * b).astype(BF16)
        o_ref[0, r, :] = xr + g2_ref[0] * _dot(u, w2_ref[...])


def _merge_ffn(attn, go, smg, x, g1, sh2, sc2, g2, n2, wa, wg, wo, w1, w2, tm):
    B, T, D = x.shape
    weights = (n2, wa, wg, wo, w1, w2)
    w_bytes = sum(int(np.prod(w.shape)) * w.dtype.itemsize for w in weights)
    tile_bytes = tm * (D * 2 + D * 2 + 2 * D * 2 + D * 4 + D * 4)
    temp_bytes = tm * (2 * D_FF * 4 + D_FF * 2 + 6 * D * 4)
    return pl.pallas_call(
        _merge_ffn_kernel,
        grid=(B, T // tm),
        in_specs=[_row_spec(tm, D), _row_spec(tm, D), _row_spec(tm, 2 * D), _row_spec(tm, D)]
        + [_vec_spec(D)] * 4 + [_const_spec(w.shape) for w in weights],
        out_specs=_row_spec(tm, D),
        out_shape=jax.ShapeDtypeStruct((B, T, D), F32),
        compiler_params=pltpu.CompilerParams(
            dimension_semantics=("arbitrary", "arbitrary"),
            vmem_limit_bytes=_vmem_limit(w_bytes + 2 * tile_bytes + temp_bytes),
        ),
        name="merge_ffn",
    )(attn, go, smg, x, g1, sh2, sc2, g2, n2, wa, wg, wo, w1, w2)


def _deinterleave_heads(w, heads):
    rows = w.shape[0]
    return w.reshape(rows, heads, HEAD_DIM // 2, 2).swapaxes(2, 3).reshape(rows, heads * HEAD_DIM)


def _deinterleave_perm(heads):
    one = np.concatenate([np.arange(0, HEAD_DIM, 2), np.arange(1, HEAD_DIM, 2)])
    return np.concatenate([h * HEAD_DIM + one for h in range(heads)])


def _rope_tables(T):
    rows = T // GRID_W
    row = jnp.repeat(jnp.arange(rows, dtype=F32), GRID_W)
    col = jnp.tile(jnp.arange(GRID_W, dtype=F32), rows)
    half = HEAD_DIM // 2
    inv_freq = 1.0 / (ROPE_THETA ** (jnp.arange(0, half, 2, dtype=F32) / half))
    ang = jnp.concatenate([row[:, None] * inv_freq[None], col[:, None] * inv_freq[None]], axis=-1)
    cos, sin = jnp.cos(ang), jnp.sin(ang)
    return jnp.concatenate([cos, cos], axis=-1), jnp.concatenate([-sin, sin], axis=-1)


def _block(x, c, ctx, c_ctx, w_ada, b_ada, norm1_g, w_in, q_norm_g, k_norm_g, gk_up_f, gk_up_f_b, gk_up_b, gk_up_b_b,
           gla_norm_g, w_attn_proj, w_gla_proj, w_out, norm2_g, w_ffn_in, w_ffn_out):
    B, T, D = x.shape
    Tc = ctx.shape[1]

    mod = _adaln(jnp.concatenate([c, c_ctx[None, :]], axis=0), w_ada, b_ada)
    sh1, sc1, g1, sh2, sc2, g2 = [m[:B, None, :] for m in jnp.split(mod, 6, axis=-1)]
    sh1c, sc1c = [jnp.broadcast_to(m[B:, None, :], (B, 1, D)) for m in jnp.split(mod, 6, axis=-1)[:2]]

    offs = np.concatenate([[0], np.cumsum(IN_WIDTHS)])
    cols = [w_in[:, offs[i]:offs[i + 1]] for i in range(len(IN_WIDTHS))]
    w_q = _deinterleave_heads(cols[0].astype(BF16), N_Q_HEADS)
    w_k = _deinterleave_heads(cols[1].astype(BF16), N_KV_HEADS)
    wqkv = jnp.concatenate([w_q, w_k, cols[2].astype(BF16)], axis=1)
    wgla = jnp.concatenate([c_.astype(BF16) for c_ in cols[3:7]], axis=1)
    wlr = jnp.pad(cols[7], ((0, 0), (0, LR_PAD - 2 * GLA_LOWRANK))).astype(BF16)
    wmg = cols[8].astype(BF16)
    up = jnp.zeros((LR_PAD, 2 * GLA_QK_W), F32)
    up = up.at[:GLA_LOWRANK, :GLA_QK_W].set(gk_up_f).at[GLA_LOWRANK:2 * GLA_LOWRANK, GLA_QK_W:].set(gk_up_b).astype(BF16)
    upb = jnp.concatenate([gk_up_f_b, gk_up_b_b])[None, :]
    perm1 = _deinterleave_perm(1)
    qg = (q_norm_g[perm1] * (HEAD_DIM ** -0.5 * LOG2_E))[None, :]
    kg = k_norm_g[perm1][None, :]
    n1 = norm1_g[None, :]
    cosf, sinf = _rope_tables(T)

    k_c, v_c, gk_c, gv_c, la_c = _inproj_context(ctx, sh1c, sc1c, n1, wqkv, wgla, wlr, up, upb, kg, tm=min(Tc, 256))
    q, k_x, v_x, gq, gk, gv, sgg, la, smg = _inproj_latent(
        x, sh1, sc1, n1, wqkv, wgla, wlr, wmg, up, upb, qg, kg, cosf, sinf, tm=min(T, 512))

    attn = _attention(q, k_c, v_c, k_x, v_x, tq=min(T, 1024))
    go = _gla(gk_c, gv_c, la_c, gq, gk, gv, la, sgg, gla_norm_g[None, :])
    return _merge_ffn(attn, go, smg, x, g1, sh2, sc2, g2, norm2_g[None, :], w_attn_proj.astype(BF16),
                      w_gla_proj.astype(BF16), w_out.astype(BF16), w_ffn_in.astype(BF16), w_ffn_out.astype(BF16),
                      tm=min(T, 512))


def kernel(x, c, ctx, c_ctx, w_ada, b_ada, norm1_g, w_in, q_norm_g, k_norm_g, gk_up_f, gk_up_f_b, gk_up_b, gk_up_b_b,
           gla_norm_g, w_attn_proj, w_gla_proj, w_out, norm2_g, w_ffn_in, w_ffn_out):
    assert w_ada.shape[0] == 1, "single trunk layer"
    return _block(x, c, ctx, c_ctx, w_ada[0], b_ada[0], norm1_g[0], w_in[0], q_norm_g[0], k_norm_g[0], gk_up_f[0],
                  gk_up_f_b[0], gk_up_b[0], gk_up_b_b[0], gla_norm_g[0], w_attn_proj[0], w_gla_proj[0], w_out[0],
                  norm2_g[0], w_ffn_in[0], w_ffn_out[0])
```

```python
import functools

import numpy as np
import jax
import jax.numpy as jnp
from jax import lax
from jax.experimental import pallas as pl
from jax.experimental.pallas import tpu as pltpu

F32 = jnp.float32
BF16 = jnp.bfloat16

D_MODEL = 1024
GRID_W = 64
HEAD_DIM = 128
N_Q_HEADS = 8
N_KV_HEADS = 2
Q_PER_KV = N_Q_HEADS // N_KV_HEADS
ROPE_THETA = 10000.0
GLA_HEADS = 4
GLA_DK = 128
GLA_DV = 256
GLA_LOWRANK = 16
GLA_GATE_NORM = 16.0
GLA_CHUNK = 64
D_FF = 2816
EPS = 1e-6
LOG2_E = 1.4426950408889634
MERGE_ROWS = 256
INPROJ_ROWS = 256
SCAN_ROWS = 256
ATTN_ROWS = 256
GLA_UNROLL = 2

ATTN_Q_W = N_Q_HEADS * HEAD_DIM
ATTN_KV_W = N_KV_HEADS * HEAD_DIM
GLA_QK_W = GLA_HEADS * GLA_DK
GLA_V_W = GLA_HEADS * GLA_DV
IN_WIDTHS = (ATTN_Q_W, ATTN_KV_W, ATTN_KV_W, GLA_QK_W, GLA_QK_W, GLA_V_W, GLA_V_W, 2 * GLA_LOWRANK, 2 * D_MODEL)
LANES = 128
LR_PAD = LANES
V7X_VMEM_CAP = 56 * 1024 * 1024


def _vmem_limit(nbytes):
    return int(min(max(nbytes, 16 * 1024 * 1024), V7X_VMEM_CAP))


def _const_spec(shape):
    nd = len(shape)
    return pl.BlockSpec(shape, lambda *_: (0,) * nd, pipeline_mode=pl.Buffered(1))


def _rms(x):
    return x * lax.rsqrt(jnp.mean(x * x, axis=-1, keepdims=True) + EPS)


def _sigmoid(x):
    return 1.0 / (1.0 + jnp.exp(-x))


def _log_sigmoid(z):
    return jnp.minimum(z, 0.0) - jnp.log(1.0 + jnp.exp(-jnp.abs(z)))


def _dot(a, b):
    return jnp.dot(a, b, preferred_element_type=F32)


def _dot_nt(a, b):
    return lax.dot_general(a, b, (((1,), (1,)), ((), ())), preferred_element_type=F32)


def _dot_tn(a, b):
    return lax.dot_general(a, b, (((0,), (0,)), ((), ())), preferred_element_type=F32)


def _adaln_kernel(c_ref, w_ref, b_ref, o_ref):
    c = c_ref[...]
    s = (c * _sigmoid(c)).astype(BF16)
    o_ref[...] = _dot(s, w_ref[...].astype(BF16)) + b_ref[...]


def _adaln(cvecs, w_ada, b_ada):
    rows, d = cvecs.shape
    n = w_ada.shape[1]
    bn = n // 4
    return pl.pallas_call(
        _adaln_kernel,
        grid=(n // bn,),
        in_specs=[
            pl.BlockSpec((rows, d), lambda j: (0, 0)),
            pl.BlockSpec((d, bn), lambda j: (0, j)),
            pl.BlockSpec((1, bn), lambda j: (0, j)),
        ],
        out_specs=pl.BlockSpec((rows, bn), lambda j: (0, j)),
        out_shape=jax.ShapeDtypeStruct((rows, n), F32),
        compiler_params=pltpu.CompilerParams(
            dimension_semantics=("arbitrary",),
            vmem_limit_bytes=_vmem_limit(3 * d * bn * 4 + 8 * rows * bn * 4),
        ),
        name="adaln",
    )(cvecs, w_ada, b_ada.reshape(1, n))


def _modulated_norm(x, g, shift, scale):
    return (_rms(x) * g) * (1.0 + scale) + shift


def _head_norm(acc, g, heads, rope):
    outs = []
    for h in range(heads):
        y = _rms(acc[:, h * HEAD_DIM:(h + 1) * HEAD_DIM]) * g
        if rope is not None:
            cosf, sinf = rope
            y = y * cosf + pltpu.roll(y, HEAD_DIM // 2, 1) * sinf
        outs.append(y.astype(BF16))
    return outs


def _log_decays(lr, up_ref, upb_ref):
    z = _dot(lr, up_ref[...]) + upb_ref[...]
    return _log_sigmoid(z) * (1.0 / GLA_GATE_NORM)


def _inproj_latent_kernel(x_ref, sh_ref, sc_ref, g_ref, wqkv_ref, wgla_ref, wlr_ref, wmg_ref, up_ref, upb_ref,
                          qg_ref, kg_ref, cos_ref, sin_ref,
                          q_ref, k_ref, v_ref, gq_ref, gk_ref, gv_ref, sgg_ref, la_ref, smg_ref):
    for r0 in range(0, x_ref.shape[1], INPROJ_ROWS):
        rows = slice(r0, r0 + INPROJ_ROWS)
        hb = _modulated_norm(x_ref[0, rows, :], g_ref[...], sh_ref[0], sc_ref[0]).astype(BF16)
        rope = (cos_ref[rows, :], sin_ref[rows, :])
        lr = _dot(hb, wlr_ref[...]).astype(BF16)
        smg_ref[0, rows, :] = _sigmoid(_dot(hb, wmg_ref[...])).astype(BF16)
        la_ref[0, rows, :] = _log_decays(lr, up_ref, upb_ref)
        gg = _dot(hb, wgla_ref[:, 2 * GLA_QK_W + GLA_V_W:])
        sgg_ref[0, rows, :] = (gg * _sigmoid(gg)).astype(BF16)
        q_acc = _dot(hb, wqkv_ref[:, :ATTN_Q_W])
        for h, y in enumerate(_head_norm(q_acc, qg_ref[...], N_Q_HEADS, rope)):
            q_ref[0, rows, h * HEAD_DIM:(h + 1) * HEAD_DIM] = y
        kv_acc = _dot(hb, wqkv_ref[:, ATTN_Q_W:])
        for h, y in enumerate(_head_norm(kv_acc[:, :ATTN_KV_W], kg_ref[...], N_KV_HEADS, rope)):
            k_ref[0, rows, h * HEAD_DIM:(h + 1) * HEAD_DIM] = y
        v_ref[0, rows, :] = kv_acc[:, ATTN_KV_W:].astype(BF16)
        gqk = _dot(hb, wgla_ref[:, :2 * GLA_QK_W])
        gq_ref[0, rows, :] = gqk[:, :GLA_QK_W].astype(BF16)
        gk_ref[0, rows, :] = gqk[:, GLA_QK_W:].astype(BF16)
        gv_ref[0, rows, :] = _dot(hb, wgla_ref[:, 2 * GLA_QK_W:2 * GLA_QK_W + GLA_V_W]).astype(BF16)


def _inproj_context_kernel(x_ref, sh_ref, sc_ref, g_ref, wkv_ref, wgla_ref, wlr_ref, up_ref, upb_ref, kg_ref,
                           k_ref, v_ref, gk_ref, gv_ref, la_ref):
    hb = _modulated_norm(x_ref[0], g_ref[...], sh_ref[0], sc_ref[0]).astype(BF16)
    lr = _dot(hb, wlr_ref[...]).astype(BF16)
    kv_acc = _dot(hb, wkv_ref[:, ATTN_Q_W:])
    la_ref[0] = _log_decays(lr, up_ref, upb_ref)
    for h, y in enumerate(_head_norm(kv_acc[:, :ATTN_KV_W], kg_ref[...], N_KV_HEADS, None)):
        k_ref[0, :, h * HEAD_DIM:(h + 1) * HEAD_DIM] = y
    v_ref[0] = kv_acc[:, ATTN_KV_W:].astype(BF16)
    gk_ref[0] = _dot(hb, wgla_ref[:, GLA_QK_W:2 * GLA_QK_W]).astype(BF16)
    gv_ref[0] = _dot(hb, wgla_ref[:, 2 * GLA_QK_W:2 * GLA_QK_W + GLA_V_W]).astype(BF16)


def _row_spec(tm, width):
    return pl.BlockSpec((1, tm, width), lambda b, j: (b, j, 0))


def _vec_spec(width):
    return pl.BlockSpec((1, 1, width), lambda b, j: (b, 0, 0))


def _inproj_latent(x, sh, sc, g, wqkv, wgla, wlr, wmg, up, upb, qg, kg, cosf, sinf, tm):
    B, T, D = x.shape
    out_widths = (ATTN_Q_W, ATTN_KV_W, ATTN_KV_W, GLA_QK_W, GLA_QK_W, GLA_V_W, GLA_V_W, 2 * GLA_QK_W, 2 * D_MODEL)
    out_dtypes = (BF16,) * 7 + (F32, BF16)
    weights = (g, wqkv, wgla, wlr, wmg, up, upb, qg, kg)
    w_bytes = sum(int(np.prod(w.shape)) * w.dtype.itemsize for w in weights)
    tile_bytes = tm * D * 4 + sum(tm * w * jnp.dtype(dt).itemsize for w, dt in zip(out_widths, out_dtypes))
    temp_bytes = tm * (D * 6 + 2 * D_MODEL * 4 * 3)
    return pl.pallas_call(
        _inproj_latent_kernel,
        grid=(B, T // tm),
        in_specs=[_row_spec(tm, D), _vec_spec(D), _vec_spec(D)]
        + [_const_spec(w.shape) for w in weights]
        + [pl.BlockSpec((tm, HEAD_DIM), lambda b, j: (j, 0))] * 2,
        out_specs=[_row_spec(tm, w) for w in out_widths],
        out_shape=[jax.ShapeDtypeStruct((B, T, w), dt) for w, dt in zip(out_widths, out_dtypes)],
        compiler_params=pltpu.CompilerParams(
            dimension_semantics=("arbitrary", "arbitrary"),
            vmem_limit_bytes=_vmem_limit(w_bytes + 2 * tile_bytes + temp_bytes),
        ),
        name="inproj_latent",
    )(x, sh, sc, g, wqkv, wgla, wlr, wmg, up, upb, qg, kg, cosf, sinf)


def _inproj_context(ctx, sh, sc, g, wkv, wgla, wlr, up, upb, kg, tm):
    B, Tc, D = ctx.shape
    out_widths = (ATTN_KV_W, ATTN_KV_W, GLA_QK_W, GLA_V_W, 2 * GLA_QK_W)
    out_dtypes = (BF16,) * 4 + (F32,)
    weights = (g, wkv, wgla, wlr, up, upb, kg)
    w_bytes = sum(int(np.prod(w.shape)) * w.dtype.itemsize for w in weights)
    tile_bytes = tm * D * 4 + sum(tm * w * jnp.dtype(dt).itemsize for w, dt in zip(out_widths, out_dtypes))
    temp_bytes = tm * (D * 6 + 2 * D_MODEL * 4 * 3)
    return pl.pallas_call(
        _inproj_context_kernel,
        grid=(B, Tc // tm),
        in_specs=[_row_spec(tm, D), _vec_spec(D), _vec_spec(D)] + [_const_spec(w.shape) for w in weights],
        out_specs=[_row_spec(tm, w) for w in out_widths],
        out_shape=[jax.ShapeDtypeStruct((B, Tc, w), dt) for w, dt in zip(out_widths, out_dtypes)],
        compiler_params=pltpu.CompilerParams(
            dimension_semantics=("arbitrary", "arbitrary"),
            vmem_limit_bytes=_vmem_limit(w_bytes + 2 * tile_bytes + temp_bytes),
        ),
        name="inproj_context",
    )(ctx, sh, sc, g, wkv, wgla, wlr, up, upb, kg)


def _attn_kernel(q_ref, kc_ref, vc_ref, kx_ref, vx_ref, o_ref, k_all, v_aug):
    Tc = kc_ref.shape[1]

    @pl.when(pl.program_id(2) == 0)
    def _():
        k_all[:Tc, :] = kc_ref[0]
        k_all[Tc:, :] = kx_ref[0]
        v_aug[:Tc, :HEAD_DIM] = vc_ref[0]
        v_aug[Tc:, :HEAD_DIM] = vx_ref[0]
        v_aug[:, HEAD_DIM:] = jnp.ones((v_aug.shape[0], HEAD_DIM), BF16)

    chains = [(slice(r * ATTN_ROWS, (r + 1) * ATTN_ROWS), slice(g * HEAD_DIM, (g + 1) * HEAD_DIM))
              for r in range(q_ref.shape[1] // ATTN_ROWS) for g in range(Q_PER_KV)]

    def scores(chain):
        rows, lanes = chain
        return _dot_nt(q_ref[0, rows, lanes], k_all[...])

    s_next = scores(chains[0])
    for i, (rows, lanes) in enumerate(chains):
        s = s_next
        if i + 1 < len(chains):
            s_next = scores(chains[i + 1])
        p = jnp.exp2(s - jnp.max(s, axis=-1, keepdims=True)).astype(BF16)
        oa = _dot(p, v_aug[...])
        o_ref[0, rows, lanes] = (oa[:, :HEAD_DIM] / oa[:, HEAD_DIM:]).astype(BF16)


def _attention(q, k_c, v_c, k_x, v_x, tq):
    B, T, _ = q.shape
    Tc = k_c.shape[1]
    gw = Q_PER_KV * HEAD_DIM
    score_bytes = Q_PER_KV * tq * (T + Tc) * (4 + 2)
    io_bytes = 2 * (2 * tq * gw * 2 + 2 * (T + Tc) * HEAD_DIM * 2) + (T + Tc) * 3 * HEAD_DIM * 2
    return pl.pallas_call(
        _attn_kernel,
        grid=(B, N_KV_HEADS, T // tq),
        in_specs=[
            pl.BlockSpec((1, tq, gw), lambda b, h, j: (b, j, h)),
            pl.BlockSpec((1, Tc, HEAD_DIM), lambda b, h, j: (b, 0, h)),
            pl.BlockSpec((1, Tc, HEAD_DIM), lambda b, h, j: (b, 0, h)),
            pl.BlockSpec((1, T, HEAD_DIM), lambda b, h, j: (b, 0, h)),
            pl.BlockSpec((1, T, HEAD_DIM), lambda b, h, j: (b, 0, h)),
        ],
        out_specs=pl.BlockSpec((1, tq, gw), lambda b, h, j: (b, j, h)),
        out_shape=jax.ShapeDtypeStruct((B, T, ATTN_Q_W), BF16),
        scratch_shapes=[pltpu.VMEM((Tc + T, HEAD_DIM), BF16), pltpu.VMEM((Tc + T, 2 * HEAD_DIM), BF16)],
        compiler_params=pltpu.CompilerParams(
            dimension_semantics=("arbitrary", "arbitrary", "arbitrary"),
            vmem_limit_bytes=_vmem_limit(score_bytes + io_bytes),
        ),
        name="attention",
    )(q, k_c, v_c, k_x, v_x)


def _chunk_scan(x, reverse):
    n, dk = x.shape
    g = min(n, SCAN_ROWS)
    r = lax.broadcasted_iota(jnp.int32, (g, g), 0)
    c = lax.broadcasted_iota(jnp.int32, (g, g), 1)
    same_chunk = (r // GLA_CHUNK) == (c // GLA_CHUNK)
    tri = jnp.where(same_chunk, jnp.where((c >= r) if reverse else (c <= r), 1.0, 0.0), 0.0).astype(BF16)
    hi = x.astype(BF16)
    lo = (x - hi.astype(F32)).astype(BF16)
    parts = jnp.concatenate([hi, lo], axis=1)
    sums = []
    for i in range(n // g):
        y = _dot(tri, parts[i * g:(i + 1) * g, :])
        sums.append(y[:, :dk] + y[:, dk:])
    return jnp.concatenate(sums, axis=0)


def _gla_prep(q, k, la, reverse, qe_ref, ke_ref, tot_ref, first_row):
    b = _chunk_scan(la, reverse)
    ke_ref[...] = (k.astype(F32) * jnp.exp(-b)).astype(BF16)
    if q is not None:
        qe_ref[...] = (q.astype(F32) * jnp.exp(b) * (GLA_DK ** -0.5)).astype(BF16)
    last = 0 if reverse else GLA_CHUNK - 1
    for n in range(la.shape[0] // GLA_CHUNK):
        row = n * GLA_CHUNK + last
        tot_ref[first_row + n:first_row + n + 1, :] = b[row:row + 1, :]


def _gla_kernel(kc_ref, vc_ref, lafc_ref, labc_ref, q_ref, k_ref, v_ref, laf_ref, lab_ref, sgg_ref, gn_ref,
                o_ref, acc_ref, qe_ref, ke_ref, kec_ref, tot_ref, u_ref):
    C = GLA_CHUNK
    n_ctx = kc_ref.shape[1] // C
    n_lat = q_ref.shape[1] // C

    tot_ref[...] = jnp.zeros_like(tot_ref)
    for d, (lac, la) in enumerate(((lafc_ref, laf_ref), (labc_ref, lab_ref))):
        _gla_prep(q_ref[0], k_ref[0], la[0], d == 1, qe_ref.at[d], ke_ref.at[d], tot_ref.at[d], 0)
        _gla_prep(None, kc_ref[0], lac[0], d == 1, None, kec_ref.at[d], tot_ref.at[d], n_lat)
    decay = [jnp.exp(tot_ref[d].T) for d in range(2)]

    def rows(i):
        return slice(i * C, (i + 1) * C)

    r_id = lax.broadcasted_iota(jnp.int32, (C, C), 0)
    c_id = lax.broadcasted_iota(jnp.int32, (C, C), 1)
    scores = []
    for n in range(n_lat):
        r = rows(n)
        a = (jnp.where(c_id <= r_id, _dot_nt(qe_ref[0, r, :], ke_ref[0, r, :]), 0.0)
             + jnp.where(c_id >= r_id, _dot_nt(qe_ref[1, r, :], ke_ref[1, r, :]), 0.0))
        scores.append(a.astype(BF16))

    for d in range(2):
        for n in range(n_lat):
            u_ref[d, n] = _dot_tn(ke_ref[d, rows(n), :], v_ref[0, rows(n), :])
        for n in range(n_ctx):
            u_ref[d, n_lat + n] = _dot_tn(kec_ref[d, rows(n), :], vc_ref[0, rows(n), :])

    for n in range(n_lat):
        acc_ref[rows(n), :] = _dot(scores[n], v_ref[0, rows(n), :])

    def step(state, d, n):
        col = decay[d][:, n:n + 1]
        return col * state + col * u_ref[d, n]

    s_fwd = jnp.zeros((GLA_DK, GLA_DV), F32)
    s_bwd = jnp.zeros((GLA_DK, GLA_DV), F32)
    for i in range(n_ctx):
        s_fwd = step(s_fwd, 0, n_lat + i)
        s_bwd = step(s_bwd, 1, n_lat + n_ctx - 1 - i)

    for i in range(n_lat):
        j = n_lat - 1 - i
        acc_ref[rows(i), :] += _dot(qe_ref[0, rows(i), :], s_fwd.astype(BF16))
        s_fwd = step(s_fwd, 0, i)
        acc_ref[rows(j), :] += _dot(qe_ref[1, rows(j), :], s_bwd.astype(BF16))
        s_bwd = step(s_bwd, 1, j)

    go = _rms(acc_ref[...]) * gn_ref[...] * sgg_ref[0].astype(F32)
    o_ref[0] = go.astype(BF16)


def _gla(gk_c, gv_c, la_c, gq, gk, gv, la, sgg, gn):
    B, T, _ = gq.shape
    Tc = gk_c.shape[1]
    H = GLA_HEADS
    n_lat, n_ctx = T // GLA_CHUNK, Tc // GLA_CHUNK

    def blk(t, w, off=0):
        return pl.BlockSpec((1, t, w), lambda b, h: (b, 0, h + off))

    io_bytes = 2 * (Tc * (GLA_DK * 2 + GLA_DV * 2 + 2 * GLA_DK * 4)
                    + T * (2 * GLA_DK * 2 + GLA_DV * 2 + 2 * GLA_DK * 4 + 2 * GLA_DV * 2))
    scratch = [
        pltpu.VMEM((T, GLA_DV), F32),
        pltpu.VMEM((2, T, GLA_DK), BF16),
        pltpu.VMEM((2, T, GLA_DK), BF16),
        pltpu.VMEM((2, Tc, GLA_DK), BF16),
        pltpu.VMEM((2, GLA_DK, GLA_DK), F32),
        pltpu.VMEM((2, n_lat + n_ctx, GLA_DK, GLA_DV), F32),
    ]
    assert n_lat + n_ctx <= GLA_DK
    scratch_bytes = (T * GLA_DV * 4 + 4 * (T + Tc) * GLA_DK * 2 + 2 * GLA_DK * GLA_DK * 4
                     + 2 * (n_lat + n_ctx) * GLA_DK * GLA_DV * 4)
    temp_bytes = 6 * T * GLA_DK * 4 + 3 * T * GLA_DV * 4
    return pl.pallas_call(
        _gla_kernel,
        grid=(B, H),
        in_specs=[
            blk(Tc, GLA_DK), blk(Tc, GLA_DV), blk(Tc, GLA_DK), blk(Tc, GLA_DK, H),
            blk(T, GLA_DK), blk(T, GLA_DK), blk(T, GLA_DV), blk(T, GLA_DK), blk(T, GLA_DK, H),
            blk(T, GLA_DV),
            pl.BlockSpec((1, GLA_DV), lambda b, h: (0, 0)),
        ],
        out_specs=blk(T, GLA_DV),
        out_shape=jax.ShapeDtypeStruct((B, T, GLA_V_W), BF16),
        scratch_shapes=scratch,
        compiler_params=pltpu.CompilerParams(
            dimension_semantics=("arbitrary", "arbitrary"),
            vmem_limit_bytes=_vmem_limit(io_bytes + scratch_bytes + temp_bytes),
        ),
        name="gla",
    )(gk_c, gv_c, la_c, la_c, gq, gk, gv, la, la, sgg, gn)


def _merge_ffn_kernel(attn_ref, go_ref, smg_ref, x_ref, g1_ref, sh2_ref, sc2_ref, g2_ref,
                      n2_ref, wa_ref, wg_ref, wo_ref, w1_ref, w2_ref, o_ref):
    blocks = [slice(r0, r0 + MERGE_ROWS) for r0 in range(0, x_ref.shape[1], MERGE_ROWS)]
    branch = [(_dot(attn_ref[0, r, :], wa_ref[...]), _dot(go_ref[0, r, :], wg_ref[...])) for r in blocks]
    x1 = []
    for r, (ya, yg) in zip(blocks, branch):
        smg = smg_ref[0, r, :]
        merged = smg[:, :D_MODEL].astype(F32) * ya + smg[:, D_MODEL:].astype(F32) * yg
        x1.append(x_ref[0, r, :] + g1_ref[0] * _dot(merged.astype(BF16), wo_ref[...]))
    hidden = []
    for xr in x1:
        h2 = _modulated_norm(xr, n2_ref[...], sh2_ref[0], sc2_ref[0]).astype(BF16)
        hidden.append((_dot(h2, w1_ref[:, :D_FF]), _dot(h2, w1_ref[:, D_FF:])))
    for r, xr, (a, b) in zip(blocks, x1, hidden):
        u = (a * _sigmoid(a) * b).astype(BF16)
        o_ref[0, r, :] = xr + g2_ref[0] * _dot(u, w2_ref[...])


def _merge_ffn(attn, go, smg, x, g1, sh2, sc2, g2, n2, wa, wg, wo, w1, w2, tm):
    B, T, D = x.shape
    weights = (n2, wa, wg, wo, w1, w2)
    w_bytes = sum(int(np.prod(w.shape)) * w.dtype.itemsize for w in weights)
    tile_bytes = tm * (D * 2 + D * 2 + 2 * D * 2 + D * 4 + D * 4)
    temp_bytes = tm * (2 * D_FF * 4 + D_FF * 2 + 6 * D * 4)
    return pl.pallas_call(
        _merge_ffn_kernel,
        grid=(B, T // tm),
        in_specs=[_row_spec(tm, D), _row_spec(tm, D), _row_spec(tm, 2 * D), _row_spec(tm, D)]
        + [_vec_spec(D)] * 4 + [_const_spec(w.shape) for w in weights],
        out_specs=_row_spec(tm, D),
        out_shape=jax.ShapeDtypeStruct((B, T, D), F32),
        compiler_params=pltpu.CompilerParams(
            dimension_semantics=("arbitrary", "arbitrary"),
            vmem_limit_bytes=_vmem_limit(w_bytes + 2 * tile_bytes + temp_bytes),
        ),
        name="merge_ffn",
    )(attn, go, smg, x, g1, sh2, sc2, g2, n2, wa, wg, wo, w1, w2)


def _deinterleave_heads(w, heads):
    rows = w.shape[0]
    return w.reshape(rows, heads, HEAD_DIM // 2, 2).swapaxes(2, 3).reshape(rows, heads * HEAD_DIM)


def _deinterleave_perm(heads):
    one = np.concatenate([np.arange(0, HEAD_DIM, 2), np.arange(1, HEAD_DIM, 2)])
    return np.concatenate([h * HEAD_DIM + one for h in range(heads)])


def _rope_tables(T):
    rows = T // GRID_W
    row = jnp.repeat(jnp.arange(rows, dtype=F32), GRID_W)
    col = jnp.tile(jnp.arange(GRID_W, dtype=F32), rows)
    half = HEAD_DIM // 2
    inv_freq = 1.0 / (ROPE_THETA ** (jnp.arange(0, half, 2, dtype=F32) / half))
    ang = jnp.concatenate([row[:, None] * inv_freq[None], col[:, None] * inv_freq[None]], axis=-1)
    cos, sin = jnp.cos(ang), jnp.sin(ang)
    return jnp.concatenate([cos, cos], axis=-1), jnp.concatenate([-sin, sin], axis=-1)


def _block(x, c, ctx, c_ctx, w_ada, b_ada, norm1_g, w_in, q_norm_g, k_norm_g, gk_up_f, gk_up_f_b, gk_up_b, gk_up_b_b,
           gla_norm_g, w_attn_proj, w_gla_proj, w_out, norm2_g, w_ffn_in, w_ffn_out):
    B, T, D = x.shape
    Tc = ctx.shape[1]

    mod = _adaln(jnp.concatenate([c, c_ctx[None, :]], axis=0), w_ada, b_ada)
    sh1, sc1, g1, sh2, sc2, g2 = [m[:B, None, :] for m in jnp.split(mod, 6, axis=-1)]
    sh1c, sc1c = [jnp.broadcast_to(m[B:, None, :], (B, 1, D)) for m in jnp.split(mod, 6, axis=-1)[:2]]

    offs = np.concatenate([[0], np.cumsum(IN_WIDTHS)])
    cols = [w_in[:, offs[i]:offs[i + 1]] for i in range(len(IN_WIDTHS))]
    w_q = _deinterleave_heads(cols[0].astype(BF16), N_Q_HEADS)
    w_k = _deinterleave_heads(cols[1].astype(BF16), N_KV_HEADS)
    wqkv = jnp.concatenate([w_q, w_k, cols[2].astype(BF16)], axis=1)
    wgla = jnp.concatenate([c_.astype(BF16) for c_ in cols[3:7]], axis=1)
    wlr = jnp.pad(cols[7], ((0, 0), (0, LR_PAD - 2 * GLA_LOWRANK))).astype(BF16)
    wmg = cols[8].astype(BF16)
    up = jnp.zeros((LR_PAD, 2 * GLA_QK_W), F32)
    up = up.at[:GLA_LOWRANK, :GLA_QK_W].set(gk_up_f).at[GLA_LOWRANK:2 * GLA_LOWRANK, GLA_QK_W:].set(gk_up_b).astype(BF16)
    upb = jnp.concatenate([gk_up_f_b, gk_up_b_b])[None, :]
    perm1 = _deinterleave_perm(1)
    qg = (q_norm_g[perm1] * (HEAD_DIM ** -0.5 * LOG2_E))[None, :]
    kg = k_norm_g[perm1][None, :]
    n1 = norm1_g[None, :]
    cosf, sinf = _rope_tables(T)

    k_c, v_c, gk_c, gv_c, la_c = _inproj_context(ctx, sh1c, sc1c, n1, wqkv, wgla, wlr, up, upb, kg, tm=min(Tc, 256))
    q, k_x, v_x, gq, gk, gv, sgg, la, smg = _inproj_latent(
        x, sh1, sc1, n1, wqkv, wgla, wlr, wmg, up, upb, qg, kg, cosf, sinf, tm=min(T, 512))

    attn = _attention(q, k_c, v_c, k_x, v_x, tq=min(T, 2048))
    go = _gla(gk_c, gv_c, la_c, gq, gk, gv, la, sgg, gla_norm_g[None, :])
    return _merge_ffn(attn, go, smg, x, g1, sh2, sc2, g2, norm2_g[None, :], w_attn_proj.astype(BF16),
                      w_gla_proj.astype(BF16), w_out.astype(BF16), w_ffn_in.astype(BF16), w_ffn_out.astype(BF16),
                      tm=min(T, 512))


def kernel(x, c, ctx, c_ctx, w_ada, b_ada, norm1_g, w_in, q_norm_g, k_norm_g, gk_up_f, gk_up_f_b, gk_up_b, gk_up_b_b,
           gla_norm_g, w_attn_proj, w_gla_proj, w_out, norm2_g, w_ffn_in, w_ffn_out):
    assert w_ada.shape[0] == 1, "single trunk layer"
    return _block(x, c, ctx, c_ctx, w_ada[0], b_ada[0], norm1_g[0], w_in[0], q_norm_g[0], k_norm_g[0], gk_up_f[0],
                  gk_up_f_b[0], gk_up_b[0], gk_up_b_b[0], gla_norm_g[0], w_attn_proj[0], w_gla_proj[0], w_out[0],
                  norm2_g[0], w_ffn_in[0], w_ffn_out[0])
```

```python
import numpy as np
import jax
import jax.numpy as jnp
from jax import lax
from jax.experimental import pallas as pl
from jax.experimental.pallas import tpu as pltpu

F32 = jnp.float32
BF16 = jnp.bfloat16

D_MODEL = 1024
GRID_W = 64
HEAD_DIM = 128
N_Q_HEADS = 8
N_KV_HEADS = 2
Q_PER_KV = N_Q_HEADS // N_KV_HEADS
ROPE_THETA = 10000.0
GLA_HEADS = 4
GLA_DK = 128
GLA_DV = 256
GLA_LOWRANK = 16
GLA_GATE_NORM = 16.0
GLA_CHUNK = 64
D_FF = 2816
EPS = 1e-6
LOG2_E = 1.4426950408889634
MERGE_ROWS = 256
INPROJ_ROWS = 256
SCAN_ROWS = 256
ATTN_ROWS = 512

ATTN_Q_W = N_Q_HEADS * HEAD_DIM
ATTN_KV_W = N_KV_HEADS * HEAD_DIM
GLA_QK_W = GLA_HEADS * GLA_DK
GLA_V_W = GLA_HEADS * GLA_DV
IN_WIDTHS = (ATTN_Q_W, ATTN_KV_W, ATTN_KV_W, GLA_QK_W, GLA_QK_W, GLA_V_W, GLA_V_W, 2 * GLA_LOWRANK, 2 * D_MODEL)
LANES = 128
LR_PAD = LANES
V7X_VMEM_CAP = 56 * 1024 * 1024


def _vmem_limit(nbytes):
    return int(min(max(nbytes, 16 * 1024 * 1024), V7X_VMEM_CAP))


def _const_spec(shape):
    nd = len(shape)
    return pl.BlockSpec(shape, lambda *_: (0,) * nd, pipeline_mode=pl.Buffered(1))


def _rms(x):
    return x * lax.rsqrt(jnp.mean(x * x, axis=-1, keepdims=True) + EPS)


def _sigmoid(x):
    return 1.0 / (1.0 + jnp.exp(-x))


def _log_sigmoid(z):
    return jnp.minimum(z, 0.0) - jnp.log(1.0 + jnp.exp(-jnp.abs(z)))


def _dot(a, b):
    return jnp.dot(a, b, preferred_element_type=F32)


def _dot_nt(a, b):
    return lax.dot_general(a, b, (((1,), (1,)), ((), ())), preferred_element_type=F32)


def _dot_tn(a, b):
    return lax.dot_general(a, b, (((0,), (0,)), ((), ())), preferred_element_type=F32)


def _adaln_kernel(c_ref, w_ref, b_ref, o_ref):
    c = c_ref[...]
    s = (c * _sigmoid(c)).astype(BF16)
    o_ref[...] = _dot(s, w_ref[...].astype(BF16)) + b_ref[...]


def _adaln(cvecs, w_ada, b_ada):
    rows, d = cvecs.shape
    n = w_ada.shape[1]
    bn = n // 4
    return pl.pallas_call(
        _adaln_kernel,
        grid=(n // bn,),
        in_specs=[
            pl.BlockSpec((rows, d), lambda j: (0, 0)),
            pl.BlockSpec((d, bn), lambda j: (0, j)),
            pl.BlockSpec((1, bn), lambda j: (0, j)),
        ],
        out_specs=pl.BlockSpec((rows, bn), lambda j: (0, j)),
        out_shape=jax.ShapeDtypeStruct((rows, n), F32),
        compiler_params=pltpu.CompilerParams(
            dimension_semantics=("arbitrary",),
            vmem_limit_bytes=_vmem_limit(3 * d * bn * 4 + 8 * rows * bn * 4),
        ),
        name="adaln",
    )(cvecs, w_ada, b_ada.reshape(1, n))


def _modulated_norm(x, g, shift, scale):
    return (_rms(x) * g) * (1.0 + scale) + shift


def _head_norm(acc, g, heads, rope):
    outs = []
    for h in range(heads):
        y = _rms(acc[:, h * HEAD_DIM:(h + 1) * HEAD_DIM]) * g
        if rope is not None:
            cosf, sinf = rope
            y = y * cosf + pltpu.roll(y, HEAD_DIM // 2, 1) * sinf
        outs.append(y.astype(BF16))
    return outs


def _log_decays(lr, up_ref, upb_ref):
    z = _dot(lr, up_ref[...]) + upb_ref[...]
    return _log_sigmoid(z) * (1.0 / GLA_GATE_NORM)


def _inproj_latent_kernel(x_ref, sh_ref, sc_ref, g_ref, wqkv_ref, wgla_ref, wlr_ref, wmg_ref, up_ref, upb_ref,
                          qg_ref, kg_ref, cos_ref, sin_ref,
                          q_ref, k_ref, v_ref, gq_ref, gk_ref, gv_ref, sgg_ref, la_ref, smg_ref):
    for r0 in range(0, x_ref.shape[1], INPROJ_ROWS):
        rows = slice(r0, r0 + INPROJ_ROWS)
        hb = _modulated_norm(x_ref[0, rows, :], g_ref[...], sh_ref[0], sc_ref[0]).astype(BF16)
        rope = (cos_ref[rows, :], sin_ref[rows, :])
        lr = _dot(hb, wlr_ref[...]).astype(BF16)
        smg_ref[0, rows, :] = _sigmoid(_dot(hb, wmg_ref[...])).astype(BF16)
        la_ref[0, rows, :] = _log_decays(lr, up_ref, upb_ref)
        gg = _dot(hb, wgla_ref[:, 2 * GLA_QK_W + GLA_V_W:])
        sgg_ref[0, rows, :] = (gg * _sigmoid(gg)).astype(BF16)
        q_acc = _dot(hb, wqkv_ref[:, :ATTN_Q_W])
        for h, y in enumerate(_head_norm(q_acc, qg_ref[...], N_Q_HEADS, rope)):
            q_ref[0, rows, h * HEAD_DIM:(h + 1) * HEAD_DIM] = y
        kv_acc = _dot(hb, wqkv_ref[:, ATTN_Q_W:])
        for h, y in enumerate(_head_norm(kv_acc[:, :ATTN_KV_W], kg_ref[...], N_KV_HEADS, rope)):
            k_ref[0, rows, h * HEAD_DIM:(h + 1) * HEAD_DIM] = y
        v_ref[0, rows, :] = kv_acc[:, ATTN_KV_W:].astype(BF16)
        gqk = _dot(hb, wgla_ref[:, :2 * GLA_QK_W])
        gq_ref[0, rows, :] = gqk[:, :GLA_QK_W].astype(BF16)
        gk_ref[0, rows, :] = gqk[:, GLA_QK_W:].astype(BF16)
        gv_ref[0, rows, :] = _dot(hb, wgla_ref[:, 2 * GLA_QK_W:2 * GLA_QK_W + GLA_V_W]).astype(BF16)


def _inproj_context_kernel(x_ref, sh_ref, sc_ref, g_ref, wkv_ref, wgla_ref, wlr_ref, up_ref, upb_ref, kg_ref,
                           k_ref, v_ref, gk_ref, gv_ref, la_ref):
    hb = _modulated_norm(x_ref[0], g_ref[...], sh_ref[0], sc_ref[0]).astype(BF16)
    lr = _dot(hb, wlr_ref[...]).astype(BF16)
    kv_acc = _dot(hb, wkv_ref[:, ATTN_Q_W:])
    la_ref[0] = _log_decays(lr, up_ref, upb_ref)
    for h, y in enumerate(_head_norm(kv_acc[:, :ATTN_KV_W], kg_ref[...], N_KV_HEADS, None)):
        k_ref[0, :, h * HEAD_DIM:(h + 1) * HEAD_DIM] = y
    v_ref[0] = kv_acc[:, ATTN_KV_W:].astype(BF16)
    gk_ref[0] = _dot(hb, wgla_ref[:, GLA_QK_W:2 * GLA_QK_W]).astype(BF16)
    gv_ref[0] = _dot(hb, wgla_ref[:, 2 * GLA_QK_W:2 * GLA_QK_W + GLA_V_W]).astype(BF16)


def _row_spec(tm, width):
    return pl.BlockSpec((1, tm, width), lambda b, j: (b, j, 0))


def _vec_spec(width):
    return pl.BlockSpec((1, 1, width), lambda b, j: (b, 0, 0))


def _inproj_latent(x, sh, sc, g, wqkv, wgla, wlr, wmg, up, upb, qg, kg, cosf, sinf, tm):
    B, T, D = x.shape
    out_widths = (ATTN_Q_W, ATTN_KV_W, ATTN_KV_W, GLA_QK_W, GLA_QK_W, GLA_V_W, GLA_V_W, 2 * GLA_QK_W, 2 * D_MODEL)
    out_dtypes = (BF16,) * 7 + (F32, BF16)
    weights = (g, wqkv, wgla, wlr, wmg, up, upb, qg, kg)
    w_bytes = sum(int(np.prod(w.shape)) * w.dtype.itemsize for w in weights)
    tile_bytes = tm * D * 4 + sum(tm * w * jnp.dtype(dt).itemsize for w, dt in zip(out_widths, out_dtypes))
    temp_bytes = tm * (D * 6 + 2 * D_MODEL * 4 * 3)
    return pl.pallas_call(
        _inproj_latent_kernel,
        grid=(B, T // tm),
        in_specs=[_row_spec(tm, D), _vec_spec(D), _vec_spec(D)]
        + [_const_spec(w.shape) for w in weights]
        + [pl.BlockSpec((tm, HEAD_DIM), lambda b, j: (j, 0))] * 2,
        out_specs=[_row_spec(tm, w) for w in out_widths],
        out_shape=[jax.ShapeDtypeStruct((B, T, w), dt) for w, dt in zip(out_widths, out_dtypes)],
        compiler_params=pltpu.CompilerParams(
            dimension_semantics=("arbitrary", "arbitrary"),
            vmem_limit_bytes=_vmem_limit(w_bytes + 2 * tile_bytes + temp_bytes),
        ),
        name="inproj_latent",
    )(x, sh, sc, g, wqkv, wgla, wlr, wmg, up, upb, qg, kg, cosf, sinf)


def _inproj_context(ctx, sh, sc, g, wkv, wgla, wlr, up, upb, kg, tm):
    B, Tc, D = ctx.shape
    out_widths = (ATTN_KV_W, ATTN_KV_W, GLA_QK_W, GLA_V_W, 2 * GLA_QK_W)
    out_dtypes = (BF16,) * 4 + (F32,)
    weights = (g, wkv, wgla, wlr, up, upb, kg)
    w_bytes = sum(int(np.prod(w.shape)) * w.dtype.itemsize for w in weights)
    tile_bytes = tm * D * 4 + sum(tm * w * jnp.dtype(dt).itemsize for w, dt in zip(out_widths, out_dtypes))
    temp_bytes = tm * (D * 6 + 2 * D_MODEL * 4 * 3)
    return pl.pallas_call(
        _inproj_context_kernel,
        grid=(B, Tc // tm),
        in_specs=[_row_spec(tm, D), _vec_spec(D), _vec_spec(D)] + [_const_spec(w.shape) for w in weights],
        out_specs=[_row_spec(tm, w) for w in out_widths],
        out_shape=[jax.ShapeDtypeStruct((B, Tc, w), dt) for w, dt in zip(out_widths, out_dtypes)],
        compiler_params=pltpu.CompilerParams(
            dimension_semantics=("arbitrary", "arbitrary"),
            vmem_limit_bytes=_vmem_limit(w_bytes + 2 * tile_bytes + temp_bytes),
        ),
        name="inproj_context",
    )(ctx, sh, sc, g, wkv, wgla, wlr, up, upb, kg)


def _attn_kernel(q_ref, kc_ref, vc_ref, kx_ref, vx_ref, o_ref, k_all, v_aug):
    Tc = kc_ref.shape[1]

    @pl.when(pl.program_id(2) == 0)
    def _():
        k_all[:Tc, :] = kc_ref[0]
        k_all[Tc:, :] = kx_ref[0]
        v_aug[:Tc, :HEAD_DIM] = vc_ref[0]
        v_aug[Tc:, :HEAD_DIM] = vx_ref[0]
        v_aug[:, HEAD_DIM:] = jnp.ones((v_aug.shape[0], HEAD_DIM), BF16)

    chains = [(slice(r * ATTN_ROWS, (r + 1) * ATTN_ROWS), slice(g * HEAD_DIM, (g + 1) * HEAD_DIM))
              for r in range(q_ref.shape[1] // ATTN_ROWS) for g in range(Q_PER_KV)]

    def scores(chain):
        rows, lanes = chain
        return _dot_nt(q_ref[0, rows, lanes], k_all[...])

    s_next = scores(chains[0])
    for i, (rows, lanes) in enumerate(chains):
        s = s_next
        if i + 1 < len(chains):
            s_next = scores(chains[i + 1])
        p = jnp.exp2(s - jnp.max(s, axis=-1, keepdims=True)).astype(BF16)
        oa = _dot(p, v_aug[...])
        o_ref[0, rows, lanes] = (oa[:, :HEAD_DIM] / oa[:, HEAD_DIM:]).astype(BF16)


def _attention(q, k_c, v_c, k_x, v_x, tq):
    B, T, _ = q.shape
    Tc = k_c.shape[1]
    gw = Q_PER_KV * HEAD_DIM
    score_bytes = Q_PER_KV * tq * (T + Tc) * (4 + 2)
    io_bytes = 2 * (2 * tq * gw * 2 + 2 * (T + Tc) * HEAD_DIM * 2) + (T + Tc) * 3 * HEAD_DIM * 2
    return pl.pallas_call(
        _attn_kernel,
        grid=(B, N_KV_HEADS, T // tq),
        in_specs=[
            pl.BlockSpec((1, tq, gw), lambda b, h, j: (b, j, h)),
            pl.BlockSpec((1, Tc, HEAD_DIM), lambda b, h, j: (b, 0, h)),
            pl.BlockSpec((1, Tc, HEAD_DIM), lambda b, h, j: (b, 0, h)),
            pl.BlockSpec((1, T, HEAD_DIM), lambda b, h, j: (b, 0, h)),
            pl.BlockSpec((1, T, HEAD_DIM), lambda b, h, j: (b, 0, h)),
        ],
        out_specs=pl.BlockSpec((1, tq, gw), lambda b, h, j: (b, j, h)),
        out_shape=jax.ShapeDtypeStruct((B, T, ATTN_Q_W), BF16),
        scratch_shapes=[pltpu.VMEM((Tc + T, HEAD_DIM), BF16), pltpu.VMEM((Tc + T, 2 * HEAD_DIM), BF16)],
        compiler_params=pltpu.CompilerParams(
            dimension_semantics=("arbitrary", "arbitrary", "arbitrary"),
            vmem_limit_bytes=_vmem_limit(score_bytes + io_bytes),
        ),
        name="attention",
    )(q, k_c, v_c, k_x, v_x)


def _chunk_scan(x, reverse):
    n, dk = x.shape
    g = min(n, SCAN_ROWS)
    r = lax.broadcasted_iota(jnp.int32, (g, g), 0)
    c = lax.broadcasted_iota(jnp.int32, (g, g), 1)
    same_chunk = (r // GLA_CHUNK) == (c // GLA_CHUNK)
    tri = jnp.where(same_chunk, jnp.where((c >= r) if reverse else (c <= r), 1.0, 0.0), 0.0).astype(BF16)
    hi = x.astype(BF16)
    lo = (x - hi.astype(F32)).astype(BF16)
    parts = jnp.concatenate([hi, lo], axis=1)
    sums = []
    for i in range(n // g):
        y = _dot(tri, parts[i * g:(i + 1) * g, :])
        sums.append(y[:, :dk] + y[:, dk:])
    return jnp.concatenate(sums, axis=0)


def _gla_prep(q, k, la, reverse, qe_ref, ke_ref, tot_ref, first_row):
    b = _chunk_scan(la, reverse)
    ke_ref[...] = (k.astype(F32) * jnp.exp(-b)).astype(BF16)
    if q is not None:
        qe_ref[...] = (q.astype(F32) * jnp.exp(b) * (GLA_DK ** -0.5)).astype(BF16)
    last = 0 if reverse else GLA_CHUNK - 1
    for n in range(la.shape[0] // GLA_CHUNK):
        row = n * GLA_CHUNK + last
        tot_ref[first_row + n:first_row + n + 1, :] = b[row:row + 1, :]


def _gla_kernel(kc_ref, vc_ref, lafc_ref, labc_ref, q_ref, k_ref, v_ref, laf_ref, lab_ref, sgg_ref, gn_ref,
                o_ref, acc_ref, qe_ref, ke_ref, kec_ref, tot_ref, u_ref):
    C = GLA_CHUNK
    n_ctx = kc_ref.shape[1] // C
    n_lat = q_ref.shape[1] // C

    tot_ref[...] = jnp.zeros_like(tot_ref)
    for d, (lac, la) in enumerate(((lafc_ref, laf_ref), (labc_ref, lab_ref))):
        _gla_prep(q_ref[0], k_ref[0], la[0], d == 1, qe_ref.at[d], ke_ref.at[d], tot_ref.at[d], 0)
        _gla_prep(None, kc_ref[0], lac[0], d == 1, None, kec_ref.at[d], tot_ref.at[d], n_lat)
    decay = [jnp.exp(tot_ref[d].T) for d in range(2)]

    def rows(i):
        return slice(i * C, (i + 1) * C)

    r_id = lax.broadcasted_iota(jnp.int32, (C, C), 0)
    c_id = lax.broadcasted_iota(jnp.int32, (C, C), 1)
    scores = []
    for n in range(n_lat):
        r = rows(n)
        a = (jnp.where(c_id <= r_id, _dot_nt(qe_ref[0, r, :], ke_ref[0, r, :]), 0.0)
             + jnp.where(c_id >= r_id, _dot_nt(qe_ref[1, r, :], ke_ref[1, r, :]), 0.0))
        scores.append(a.astype(BF16))

    for d in range(2):
        for n in range(n_lat):
            u_ref[d, n] = _dot_tn(ke_ref[d, rows(n), :], v_ref[0, rows(n), :])
        for n in range(n_ctx):
            u_ref[d, n_lat + n] = _dot_tn(kec_ref[d, rows(n), :], vc_ref[0, rows(n), :])

    for n in range(n_lat):
        acc_ref[rows(n), :] = _dot(scores[n], v_ref[0, rows(n), :])

    def step(state, d, n):
        col = decay[d][:, n:n + 1]
        return col * state + col * u_ref[d, n]

    s_fwd = jnp.zeros((GLA_DK, GLA_DV), F32)
    s_bwd = jnp.zeros((GLA_DK, GLA_DV), F32)
    for i in range(n_ctx):
        s_fwd = step(s_fwd, 0, n_lat + i)
        s_bwd = step(s_bwd, 1, n_lat + n_ctx - 1 - i)

    for i in range(n_lat):
        j = n_lat - 1 - i
        acc_ref[rows(i), :] += _dot(qe_ref[0, rows(i), :], s_fwd.astype(BF16))
        s_fwd = step(s_fwd, 0, i)
        acc_ref[rows(j), :] += _dot(qe_ref[1, rows(j), :], s_bwd.astype(BF16))
        s_bwd = step(s_bwd, 1, j)

    go = _rms(acc_ref[...]) * gn_ref[...] * sgg_ref[0].astype(F32)
    o_ref[0] = go.astype(BF16)


def _gla(gk_c, gv_c, la_c, gq, gk, gv, la, sgg, gn):
    B, T, _ = gq.shape
    Tc = gk_c.shape[1]
    H = GLA_HEADS
    n_lat, n_ctx = T // GLA_CHUNK, Tc // GLA_CHUNK

    def blk(t, w, off=0):
        return pl.BlockSpec((1, t, w), lambda b, h: (b, 0, h + off))

    io_bytes = 2 * (Tc * (GLA_DK * 2 + GLA_DV * 2 + 2 * GLA_DK * 4)
                    + T * (2 * GLA_DK * 2 + GLA_DV * 2 + 2 * GLA_DK * 4 + 2 * GLA_DV * 2))
    scratch = [
        pltpu.VMEM((T, GLA_DV), F32),
        pltpu.VMEM((2, T, GLA_DK), BF16),
        pltpu.VMEM((2, T, GLA_DK), BF16),
        pltpu.VMEM((2, Tc, GLA_DK), BF16),
        pltpu.VMEM((2, GLA_DK, GLA_DK), F32),
        pltpu.VMEM((2, n_lat + n_ctx, GLA_DK, GLA_DV), F32),
    ]
    assert n_lat + n_ctx <= GLA_DK
    scratch_bytes = (T * GLA_DV * 4 + 4 * (T + Tc) * GLA_DK * 2 + 2 * GLA_DK * GLA_DK * 4
                     + 2 * (n_lat + n_ctx) * GLA_DK * GLA_DV * 4)
    temp_bytes = 6 * T * GLA_DK * 4 + 3 * T * GLA_DV * 4
    return pl.pallas_call(
        _gla_kernel,
        grid=(B, H),
        in_specs=[
            blk(Tc, GLA_DK), blk(Tc, GLA_DV), blk(Tc, GLA_DK), blk(Tc, GLA_DK, H),
            blk(T, GLA_DK), blk(T, GLA_DK), blk(T, GLA_DV), blk(T, GLA_DK), blk(T, GLA_DK, H),
            blk(T, GLA_DV),
            pl.BlockSpec((1, GLA_DV), lambda b, h: (0, 0)),
        ],
        out_specs=blk(T, GLA_DV),
        out_shape=jax.ShapeDtypeStruct((B, T, GLA_V_W), BF16),
        scratch_shapes=scratch,
        compiler_params=pltpu.CompilerParams(
            dimension_semantics=("arbitrary", "arbitrary"),
            vmem_limit_bytes=_vmem_limit(io_bytes + scratch_bytes + temp_bytes),
        ),
        name="gla",
    )(gk_c, gv_c, la_c, la_c, gq, gk, gv, la, la, sgg, gn)


def _merge_ffn_kernel(attn_ref, go_ref, smg_ref, x_ref, g1_ref, sh2_ref, sc2_ref, g2_ref,
                      n2_ref, wa_ref, wg_ref, wo_ref, w1_ref, w2_ref, o_ref):
    blocks = [slice(r0, r0 + MERGE_ROWS) for r0 in range(0, x_ref.shape[1], MERGE_ROWS)]
    branch = [(_dot(attn_ref[0, r, :], wa_ref[...]), _dot(go_ref[0, r, :], wg_ref[...])) for r in blocks]
    x1 = []
    for r, (ya, yg) in zip(blocks, branch):
        smg = smg_ref[0, r, :]
        merged = smg[:, :D_MODEL].astype(F32) * ya + smg[:, D_MODEL:].astype(F32) * yg
        x1.append(x_ref[0, r, :] + g1_ref[0] * _dot(merged.astype(BF16), wo_ref[...]))
    hidden = []
    for xr in x1:
        h2 = _modulated_norm(xr, n2_ref[...], sh2_ref[0], sc2_ref[0]).astype(BF16)
        hidden.append((_dot(h2, w1_ref[:, :D_FF]), _dot(h2, w1_ref[:, D_FF:])))
    for r, xr, (a, b) in zip(blocks, x1, hidden):
        u = (a * _sigmoid(a) * b).astype(BF16)
        o_ref[0, r, :] = xr + g2_ref[0] * _dot(u, w2_ref[...])


def _merge_ffn(attn, go, smg, x, g1, sh2, sc2, g2, n2, wa, wg, wo, w1, w2, tm):
    B, T, D = x.shape
    weights = (n2, wa, wg, wo, w1, w2)
    w_bytes = sum(int(np.prod(w.shape)) * w.dtype.itemsize for w in weights)
    tile_bytes = tm * (D * 2 + D * 2 + 2 * D * 2 + D * 4 + D * 4)
    temp_bytes = tm * (2 * D_FF * 4 + D_FF * 2 + 6 * D * 4)
    return pl.pallas_call(
        _merge_ffn_kernel,
        grid=(B, T // tm),
        in_specs=[_row_spec(tm, D), _row_spec(tm, D), _row_spec(tm, 2 * D), _row_spec(tm, D)]
        + [_vec_spec(D)] * 4 + [_const_spec(w.shape) for w in weights],
        out_specs=_row_spec(tm, D),
        out_shape=jax.ShapeDtypeStruct((B, T, D), F32),
        compiler_params=pltpu.CompilerParams(
            dimension_semantics=("arbitrary", "arbitrary"),
            vmem_limit_bytes=_vmem_limit(w_bytes + 2 * tile_bytes + temp_bytes),
        ),
        name="merge_ffn",
    )(attn, go, smg, x, g1, sh2, sc2, g2, n2, wa, wg, wo, w1, w2)


def _deinterleave_heads(w, heads):
    rows = w.shape[0]
    return w.reshape(rows, heads, HEAD_DIM // 2, 2).swapaxes(2, 3).reshape(rows, heads * HEAD_DIM)


def _deinterleave_perm(heads):
    one = np.concatenate([np.arange(0, HEAD_DIM, 2), np.arange(1, HEAD_DIM, 2)])
    return np.concatenate([h * HEAD_DIM + one for h in range(heads)])


def _rope_tables(T):
    rows = T // GRID_W
    row = jnp.repeat(jnp.arange(rows, dtype=F32), GRID_W)
    col = jnp.tile(jnp.arange(GRID_W, dtype=F32), rows)
    half = HEAD_DIM // 2
    inv_freq = 1.0 / (ROPE_THETA ** (jnp.arange(0, half, 2, dtype=F32) / half))
    ang = jnp.concatenate([row[:, None] * inv_freq[None], col[:, None] * inv_freq[None]], axis=-1)
    cos, sin = jnp.cos(ang), jnp.sin(ang)
    return jnp.concatenate([cos, cos], axis=-1), jnp.concatenate([-sin, sin], axis=-1)


def _block(x, c, ctx, c_ctx, w_ada, b_ada, norm1_g, w_in, q_norm_g, k_norm_g, gk_up_f, gk_up_f_b, gk_up_b, gk_up_b_b,
           gla_norm_g, w_attn_proj, w_gla_proj, w_out, norm2_g, w_ffn_in, w_ffn_out):
    B, T, D = x.shape
    Tc = ctx.shape[1]

    mod = _adaln(jnp.concatenate([c, c_ctx[None, :]], axis=0), w_ada, b_ada)
    sh1, sc1, g1, sh2, sc2, g2 = [m[:B, None, :] for m in jnp.split(mod, 6, axis=-1)]
    sh1c, sc1c = [jnp.broadcast_to(m[B:, None, :], (B, 1, D)) for m in jnp.split(mod, 6, axis=-1)[:2]]

    offs = np.concatenate([[0], np.cumsum(IN_WIDTHS)])
    cols = [w_in[:, offs[i]:offs[i + 1]] for i in range(len(IN_WIDTHS))]
    w_q = _deinterleave_heads(cols[0].astype(BF16), N_Q_HEADS)
    w_k = _deinterleave_heads(cols[1].astype(BF16), N_KV_HEADS)
    wqkv = jnp.concatenate([w_q, w_k, cols[2].astype(BF16)], axis=1)
    wgla = jnp.concatenate([c_.astype(BF16) for c_ in cols[3:7]], axis=1)
    wlr = jnp.pad(cols[7], ((0, 0), (0, LR_PAD - 2 * GLA_LOWRANK))).astype(BF16)
    wmg = cols[8].astype(BF16)
    up = jnp.zeros((LR_PAD, 2 * GLA_QK_W), F32)
    up = up.at[:GLA_LOWRANK, :GLA_QK_W].set(gk_up_f).at[GLA_LOWRANK:2 * GLA_LOWRANK, GLA_QK_W:].set(gk_up_b).astype(BF16)
    upb = jnp.concatenate([gk_up_f_b, gk_up_b_b])[None, :]
    perm1 = _deinterleave_perm(1)
    qg = (q_norm_g[perm1] * (HEAD_DIM ** -0.5 * LOG2_E))[None, :]
    kg = k_norm_g[perm1][None, :]
    n1 = norm1_g[None, :]
    cosf, sinf = _rope_tables(T)

    k_c, v_c, gk_c, gv_c, la_c = _inproj_context(ctx, sh1c, sc1c, n1, wqkv, wgla, wlr, up, upb, kg, tm=min(Tc, 256))
    q, k_x, v_x, gq, gk, gv, sgg, la, smg = _inproj_latent(
        x, sh1, sc1, n1, wqkv, wgla, wlr, wmg, up, upb, qg, kg, cosf, sinf, tm=min(T, 512))

    attn = _attention(q, k_c, v_c, k_x, v_x, tq=min(T, 2048))
    go = _gla(gk_c, gv_c, la_c, gq, gk, gv, la, sgg, gla_norm_g[None, :])
    return _merge_ffn(attn, go, smg, x, g1, sh2, sc2, g2, norm2_g[None, :], w_attn_proj.astype(BF16),
                      w_gla_proj.astype(BF16), w_out.astype(BF16), w_ffn_in.astype(BF16), w_ffn_out.astype(BF16),
                      tm=min(T, 512))


def kernel(x, c, ctx, c_ctx, w_ada, b_ada, norm1_g, w_in, q_norm_g, k_norm_g, gk_up_f, gk_up_f_b, gk_up_b, gk_up_b_b,
           gla_norm_g, w_attn_proj, w_gla_proj, w_out, norm2_g, w_ffn_in, w_ffn_out):
    assert w_ada.shape[0] == 1, "single trunk layer"
    return _block(x, c, ctx, c_ctx, w_ada[0], b_ada[0], norm1_g[0], w_in[0], q_norm_g[0], k_norm_g[0], gk_up_f[0],
                  gk_up_f_b[0], gk_up_b[0], gk_up_b_b[0], gla_norm_g[0], w_attn_proj[0], w_gla_proj[0], w_out[0],
                  norm2_g[0], w_ffn_in[0], w_ffn_out[0])
```

```python
import numpy as np
import jax
import jax.numpy as jnp
from jax import lax
from jax.experimental import pallas as pl
from jax.experimental.pallas import tpu as pltpu

F32 = jnp.float32
BF16 = jnp.bfloat16

D_MODEL = 1024
GRID_W = 64
HEAD_DIM = 128
N_Q_HEADS = 8
N_KV_HEADS = 2
Q_PER_KV = N_Q_HEADS // N_KV_HEADS
ROPE_THETA = 10000.0
GLA_HEADS = 4
GLA_DK = 128
GLA_DV = 256
GLA_LOWRANK = 16
GLA_GATE_NORM = 16.0
GLA_CHUNK = 64
D_FF = 2816
EPS = 1e-6
LOG2_E = 1.4426950408889634
MERGE_ROWS = 256
INPROJ_ROWS = 256
SCAN_ROWS = 256
ATTN_ROWS = 512
ATTN_AHEAD = 2

ATTN_Q_W = N_Q_HEADS * HEAD_DIM
ATTN_KV_W = N_KV_HEADS * HEAD_DIM
GLA_QK_W = GLA_HEADS * GLA_DK
GLA_V_W = GLA_HEADS * GLA_DV
IN_WIDTHS = (ATTN_Q_W, ATTN_KV_W, ATTN_KV_W, GLA_QK_W, GLA_QK_W, GLA_V_W, GLA_V_W, 2 * GLA_LOWRANK, 2 * D_MODEL)
LANES = 128
LR_PAD = LANES
V7X_VMEM_CAP = 56 * 1024 * 1024


def _vmem_limit(nbytes):
    return int(min(max(nbytes, 16 * 1024 * 1024), V7X_VMEM_CAP))


def _const_spec(shape):
    nd = len(shape)
    return pl.BlockSpec(shape, lambda *_: (0,) * nd, pipeline_mode=pl.Buffered(1))


def _rms(x):
    return x * lax.rsqrt(jnp.mean(x * x, axis=-1, keepdims=True) + EPS)


def _sigmoid(x):
    return 1.0 / (1.0 + jnp.exp(-x))


def _log_sigmoid(z):
    return jnp.minimum(z, 0.0) - jnp.log(1.0 + jnp.exp(-jnp.abs(z)))


def _dot(a, b):
    return jnp.dot(a, b, preferred_element_type=F32)


def _dot_nt(a, b):
    return lax.dot_general(a, b, (((1,), (1,)), ((), ())), preferred_element_type=F32)


def _dot_tn(a, b):
    return lax.dot_general(a, b, (((0,), (0,)), ((), ())), preferred_element_type=F32)


def _adaln_kernel(c_ref, w_ref, b_ref, o_ref):
    c = c_ref[...]
    s = (c * _sigmoid(c)).astype(BF16)
    o_ref[...] = _dot(s, w_ref[...].astype(BF16)) + b_ref[...]


def _adaln(cvecs, w_ada, b_ada):
    rows, d = cvecs.shape
    n = w_ada.shape[1]
    bn = n // 4
    return pl.pallas_call(
        _adaln_kernel,
        grid=(n // bn,),
        in_specs=[
            pl.BlockSpec((rows, d), lambda j: (0, 0)),
            pl.BlockSpec((d, bn), lambda j: (0, j)),
            pl.BlockSpec((1, bn), lambda j: (0, j)),
        ],
        out_specs=pl.BlockSpec((rows, bn), lambda j: (0, j)),
        out_shape=jax.ShapeDtypeStruct((rows, n), F32),
        compiler_params=pltpu.CompilerParams(
            dimension_semantics=("arbitrary",),
            vmem_limit_bytes=_vmem_limit(3 * d * bn * 4 + 8 * rows * bn * 4),
        ),
        name="adaln",
    )(cvecs, w_ada, b_ada.reshape(1, n))


def _modulated_norm(x, g, shift, scale):
    return (_rms(x) * g) * (1.0 + scale) + shift


def _head_norm(acc, g, heads, rope):
    outs = []
    for h in range(heads):
        y = _rms(acc[:, h * HEAD_DIM:(h + 1) * HEAD_DIM]) * g
        if rope is not None:
            cosf, sinf = rope
            y = y * cosf + pltpu.roll(y, HEAD_DIM // 2, 1) * sinf
        outs.append(y.astype(BF16))
    return outs


def _log_decays(lr, up_ref, upb_ref):
    z = _dot(lr, up_ref[...]) + upb_ref[...]
    return _log_sigmoid(z) * (1.0 / GLA_GATE_NORM)


def _inproj_latent_kernel(x_ref, sh_ref, sc_ref, g_ref, wqkv_ref, wgla_ref, wlr_ref, wmg_ref, up_ref, upb_ref,
                          qg_ref, kg_ref, cos_ref, sin_ref,
                          q_ref, k_ref, v_ref, gq_ref, gk_ref, gv_ref, sgg_ref, la_ref, smg_ref):
    for r0 in range(0, x_ref.shape[1], INPROJ_ROWS):
        rows = slice(r0, r0 + INPROJ_ROWS)
        hb = _modulated_norm(x_ref[0, rows, :], g_ref[...], sh_ref[0], sc_ref[0]).astype(BF16)
        rope = (cos_ref[rows, :], sin_ref[rows, :])
        lr = _dot(hb, wlr_ref[...]).astype(BF16)
        smg_ref[0, rows, :] = _sigmoid(_dot(hb, wmg_ref[...])).astype(BF16)
        la_ref[0, rows, :] = _log_decays(lr, up_ref, upb_ref)
        gg = _dot(hb, wgla_ref[:, 2 * GLA_QK_W + GLA_V_W:])
        sgg_ref[0, rows, :] = (gg * _sigmoid(gg)).astype(BF16)
        q_acc = _dot(hb, wqkv_ref[:, :ATTN_Q_W])
        for h, y in enumerate(_head_norm(q_acc, qg_ref[...], N_Q_HEADS, rope)):
            q_ref[0, rows, h * HEAD_DIM:(h + 1) * HEAD_DIM] = y
        kv_acc = _dot(hb, wqkv_ref[:, ATTN_Q_W:])
        for h, y in enumerate(_head_norm(kv_acc[:, :ATTN_KV_W], kg_ref[...], N_KV_HEADS, rope)):
            k_ref[0, rows, h * HEAD_DIM:(h + 1) * HEAD_DIM] = y
        v_ref[0, rows, :] = kv_acc[:, ATTN_KV_W:].astype(BF16)
        gqk = _dot(hb, wgla_ref[:, :2 * GLA_QK_W])
        gq_ref[0, rows, :] = gqk[:, :GLA_QK_W].astype(BF16)
        gk_ref[0, rows, :] = gqk[:, GLA_QK_W:].astype(BF16)
        gv_ref[0, rows, :] = _dot(hb, wgla_ref[:, 2 * GLA_QK_W:2 * GLA_QK_W + GLA_V_W]).astype(BF16)


def _inproj_context_kernel(x_ref, sh_ref, sc_ref, g_ref, wkv_ref, wgla_ref, wlr_ref, up_ref, upb_ref, kg_ref,
                           k_ref, v_ref, gk_ref, gv_ref, la_ref):
    hb = _modulated_norm(x_ref[0], g_ref[...], sh_ref[0], sc_ref[0]).astype(BF16)
    lr = _dot(hb, wlr_ref[...]).astype(BF16)
    kv_acc = _dot(hb, wkv_ref[:, ATTN_Q_W:])
    la_ref[0] = _log_decays(lr, up_ref, upb_ref)
    for h, y in enumerate(_head_norm(kv_acc[:, :ATTN_KV_W], kg_ref[...], N_KV_HEADS, None)):
        k_ref[0, :, h * HEAD_DIM:(h + 1) * HEAD_DIM] = y
    v_ref[0] = kv_acc[:, ATTN_KV_W:].astype(BF16)
    gk_ref[0] = _dot(hb, wgla_ref[:, GLA_QK_W:2 * GLA_QK_W]).astype(BF16)
    gv_ref[0] = _dot(hb, wgla_ref[:, 2 * GLA_QK_W:2 * GLA_QK_W + GLA_V_W]).astype(BF16)


def _row_spec(tm, width):
    return pl.BlockSpec((1, tm, width), lambda b, j: (b, j, 0))


def _vec_spec(width):
    return pl.BlockSpec((1, 1, width), lambda b, j: (b, 0, 0))


def _inproj_latent(x, sh, sc, g, wqkv, wgla, wlr, wmg, up, upb, qg, kg, cosf, sinf, tm):
    B, T, D = x.shape
    out_widths = (ATTN_Q_W, ATTN_KV_W, ATTN_KV_W, GLA_QK_W, GLA_QK_W, GLA_V_W, GLA_V_W, 2 * GLA_QK_W, 2 * D_MODEL)
    out_dtypes = (BF16,) * 7 + (F32, BF16)
    weights = (g, wqkv, wgla, wlr, wmg, up, upb, qg, kg)
    w_bytes = sum(int(np.prod(w.shape)) * w.dtype.itemsize for w in weights)
    tile_bytes = tm * D * 4 + sum(tm * w * jnp.dtype(dt).itemsize for w, dt in zip(out_widths, out_dtypes))
    temp_bytes = tm * (D * 6 + 2 * D_MODEL * 4 * 3)
    return pl.pallas_call(
        _inproj_latent_kernel,
        grid=(B, T // tm),
        in_specs=[_row_spec(tm, D), _vec_spec(D), _vec_spec(D)]
        + [_const_spec(w.shape) for w in weights]
        + [pl.BlockSpec((tm, HEAD_DIM), lambda b, j: (j, 0))] * 2,
        out_specs=[_row_spec(tm, w) for w in out_widths],
        out_shape=[jax.ShapeDtypeStruct((B, T, w), dt) for w, dt in zip(out_widths, out_dtypes)],
        compiler_params=pltpu.CompilerParams(
            dimension_semantics=("arbitrary", "arbitrary"),
            vmem_limit_bytes=_vmem_limit(w_bytes + 2 * tile_bytes + temp_bytes),
        ),
        name="inproj_latent",
    )(x, sh, sc, g, wqkv, wgla, wlr, wmg, up, upb, qg, kg, cosf, sinf)


def _inproj_context(ctx, sh, sc, g, wkv, wgla, wlr, up, upb, kg, tm):
    B, Tc, D = ctx.shape
    out_widths = (ATTN_KV_W, ATTN_KV_W, GLA_QK_W, GLA_V_W, 2 * GLA_QK_W)
    out_dtypes = (BF16,) * 4 + (F32,)
    weights = (g, wkv, wgla, wlr, up, upb, kg)
    w_bytes = sum(int(np.prod(w.shape)) * w.dtype.itemsize for w in weights)
    tile_bytes = tm * D * 4 + sum(tm * w * jnp.dtype(dt).itemsize for w, dt in zip(out_widths, out_dtypes))
    temp_bytes = tm * (D * 6 + 2 * D_MODEL * 4 * 3)
    return pl.pallas_call(
        _inproj_context_kernel,
        grid=(B, Tc // tm),
        in_specs=[_row_spec(tm, D), _vec_spec(D), _vec_spec(D)] + [_const_spec(w.shape) for w in weights],
        out_specs=[_row_spec(tm, w) for w in out_widths],
        out_shape=[jax.ShapeDtypeStruct((B, Tc, w), dt) for w, dt in zip(out_widths, out_dtypes)],
        compiler_params=pltpu.CompilerParams(
            dimension_semantics=("arbitrary", "arbitrary"),
            vmem_limit_bytes=_vmem_limit(w_bytes + 2 * tile_bytes + temp_bytes),
        ),
        name="inproj_context",
    )(ctx, sh, sc, g, wkv, wgla, wlr, up, upb, kg)


def _attn_kernel(q_ref, kc_ref, vc_ref, kx_ref, vx_ref, o_ref, k_all, v_aug):
    Tc = kc_ref.shape[1]

    @pl.when(pl.program_id(2) == 0)
    def _():
        k_all[:Tc, :] = kc_ref[0]
        k_all[Tc:, :] = kx_ref[0]
        v_aug[:Tc, :HEAD_DIM] = vc_ref[0]
        v_aug[Tc:, :HEAD_DIM] = vx_ref[0]
        v_aug[:, HEAD_DIM:] = jnp.ones((v_aug.shape[0], HEAD_DIM), BF16)

    chains = [(slice(r * ATTN_ROWS, (r + 1) * ATTN_ROWS), slice(g * HEAD_DIM, (g + 1) * HEAD_DIM))
              for r in range(q_ref.shape[1] // ATTN_ROWS) for g in range(Q_PER_KV)]

    def scores(chain):
        rows, lanes = chain
        return _dot_nt(q_ref[0, rows, lanes], k_all[...])

    pending = [scores(c) for c in chains[:ATTN_AHEAD]]
    for i, (rows, lanes) in enumerate(chains):
        s = pending.pop(0)
        if i + ATTN_AHEAD < len(chains):
            pending.append(scores(chains[i + ATTN_AHEAD]))
        p = jnp.exp2(s - jnp.max(s, axis=-1, keepdims=True)).astype(BF16)
        oa = _dot(p, v_aug[...])
        o_ref[0, rows, lanes] = (oa[:, :HEAD_DIM] / oa[:, HEAD_DIM:]).astype(BF16)


def _attention(q, k_c, v_c, k_x, v_x, tq):
    B, T, _ = q.shape
    Tc = k_c.shape[1]
    gw = Q_PER_KV * HEAD_DIM
    score_bytes = Q_PER_KV * tq * (T + Tc) * (4 + 2)
    io_bytes = 2 * (2 * tq * gw * 2 + 2 * (T + Tc) * HEAD_DIM * 2) + (T + Tc) * 3 * HEAD_DIM * 2
    return pl.pallas_call(
        _attn_kernel,
        grid=(B, N_KV_HEADS, T // tq),
        in_specs=[
            pl.BlockSpec((1, tq, gw), lambda b, h, j: (b, j, h)),
            pl.BlockSpec((1, Tc, HEAD_DIM), lambda b, h, j: (b, 0, h)),
            pl.BlockSpec((1, Tc, HEAD_DIM), lambda b, h, j: (b, 0, h)),
            pl.BlockSpec((1, T, HEAD_DIM), lambda b, h, j: (b, 0, h)),
            pl.BlockSpec((1, T, HEAD_DIM), lambda b, h, j: (b, 0, h)),
        ],
        out_specs=pl.BlockSpec((1, tq, gw), lambda b, h, j: (b, j, h)),
        out_shape=jax.ShapeDtypeStruct((B, T, ATTN_Q_W), BF16),
        scratch_shapes=[pltpu.VMEM((Tc + T, HEAD_DIM), BF16), pltpu.VMEM((Tc + T, 2 * HEAD_DIM), BF16)],
        compiler_params=pltpu.CompilerParams(
            dimension_semantics=("arbitrary", "arbitrary", "arbitrary"),
            vmem_limit_bytes=_vmem_limit(score_bytes + io_bytes),
        ),
        name="attention",
    )(q, k_c, v_c, k_x, v_x)


def _chunk_scan(x, reverse):
    n, dk = x.shape
    g = min(n, SCAN_ROWS)
    r = lax.broadcasted_iota(jnp.int32, (g, g), 0)
    c = lax.broadcasted_iota(jnp.int32, (g, g), 1)
    same_chunk = (r // GLA_CHUNK) == (c // GLA_CHUNK)
    tri = jnp.where(same_chunk, jnp.where((c >= r) if reverse else (c <= r), 1.0, 0.0), 0.0).astype(BF16)
    hi = x.astype(BF16)
    lo = (x - hi.astype(F32)).astype(BF16)
    parts = jnp.concatenate([hi, lo], axis=1)
    sums = []
    for i in range(n // g):
        y = _dot(tri, parts[i * g:(i + 1) * g, :])
        sums.append(y[:, :dk] + y[:, dk:])
    return jnp.concatenate(sums, axis=0)


def _gla_prep(q, k, la, reverse, qe_ref, ke_ref, tot_ref, first_row):
    b = _chunk_scan(la, reverse)
    ke_ref[...] = (k.astype(F32) * jnp.exp(-b)).astype(BF16)
    if q is not None:
        qe_ref[...] = (q.astype(F32) * jnp.exp(b) * (GLA_DK ** -0.5)).astype(BF16)
    last = 0 if reverse else GLA_CHUNK - 1
    for n in range(la.shape[0] // GLA_CHUNK):
        row = n * GLA_CHUNK + last
        tot_ref[first_row + n:first_row + n + 1, :] = b[row:row + 1, :]


def _gla_kernel(kc_ref, vc_ref, lafc_ref, labc_ref, q_ref, k_ref, v_ref, laf_ref, lab_ref, sgg_ref, gn_ref,
                o_ref, acc_ref, qe_ref, ke_ref, kec_ref, tot_ref, u_ref):
    C = GLA_CHUNK
    n_ctx = kc_ref.shape[1] // C
    n_lat = q_ref.shape[1] // C

    tot_ref[...] = jnp.zeros_like(tot_ref)
    for d, (lac, la) in enumerate(((lafc_ref, laf_ref), (labc_ref, lab_ref))):
        _gla_prep(q_ref[0], k_ref[0], la[0], d == 1, qe_ref.at[d], ke_ref.at[d], tot_ref.at[d], 0)
        _gla_prep(None, kc_ref[0], lac[0], d == 1, None, kec_ref.at[d], tot_ref.at[d], n_lat)
    decay = [jnp.exp(tot_ref[d].T) for d in range(2)]

    def rows(i):
        return slice(i * C, (i + 1) * C)

    r_id = lax.broadcasted_iota(jnp.int32, (C, C), 0)
    c_id = lax.broadcasted_iota(jnp.int32, (C, C), 1)
    scores = []
    for n in range(n_lat):
        r = rows(n)
        a = (jnp.where(c_id <= r_id, _dot_nt(qe_ref[0, r, :], ke_ref[0, r, :]), 0.0)
             + jnp.where(c_id >= r_id, _dot_nt(qe_ref[1, r, :], ke_ref[1, r, :]), 0.0))
        scores.append(a.astype(BF16))

    for d in range(2):
        for n in range(n_lat):
            u_ref[d, n] = _dot_tn(ke_ref[d, rows(n), :], v_ref[0, rows(n), :])
        for n in range(n_ctx):
            u_ref[d, n_lat + n] = _dot_tn(kec_ref[d, rows(n), :], vc_ref[0, rows(n), :])

    for n in range(n_lat):
        acc_ref[rows(n), :] = _dot(scores[n], v_ref[0, rows(n), :])

    def step(state, d, n):
        col = decay[d][:, n:n + 1]
        return col * state + col * u_ref[d, n]

    s_fwd = jnp.zeros((GLA_DK, GLA_DV), F32)
    s_bwd = jnp.zeros((GLA_DK, GLA_DV), F32)
    for i in range(n_ctx):
        s_fwd = step(s_fwd, 0, n_lat + i)
        s_bwd = step(s_bwd, 1, n_lat + n_ctx - 1 - i)

    for i in range(n_lat):
        j = n_lat - 1 - i
        acc_ref[rows(i), :] += _dot(qe_ref[0, rows(i), :], s_fwd.astype(BF16))
        s_fwd = step(s_fwd, 0, i)
        acc_ref[rows(j), :] += _dot(qe_ref[1, rows(j), :], s_bwd.astype(BF16))
        s_bwd = step(s_bwd, 1, j)

    go = _rms(acc_ref[...]) * gn_ref[...] * sgg_ref[0].astype(F32)
    o_ref[0] = go.astype(BF16)


def _gla(gk_c, gv_c, la_c, gq, gk, gv, la, sgg, gn):
    B, T, _ = gq.shape
    Tc = gk_c.shape[1]
    H = GLA_HEADS
    n_lat, n_ctx = T // GLA_CHUNK, Tc // GLA_CHUNK

    def blk(t, w, off=0):
        return pl.BlockSpec((1, t, w), lambda b, h: (b, 0, h + off))

    io_bytes = 2 * (Tc * (GLA_DK * 2 + GLA_DV * 2 + 2 * GLA_DK * 4)
                    + T * (2 * GLA_DK * 2 + GLA_DV * 2 + 2 * GLA_DK * 4 + 2 * GLA_DV * 2))
    scratch = [
        pltpu.VMEM((T, GLA_DV), F32),
        pltpu.VMEM((2, T, GLA_DK), BF16),
        pltpu.VMEM((2, T, GLA_DK), BF16),
        pltpu.VMEM((2, Tc, GLA_DK), BF16),
        pltpu.VMEM((2, GLA_DK, GLA_DK), F32),
        pltpu.VMEM((2, n_lat + n_ctx, GLA_DK, GLA_DV), F32),
    ]
    assert n_lat + n_ctx <= GLA_DK
    scratch_bytes = (T * GLA_DV * 4 + 4 * (T + Tc) * GLA_DK * 2 + 2 * GLA_DK * GLA_DK * 4
                     + 2 * (n_lat + n_ctx) * GLA_DK * GLA_DV * 4)
    temp_bytes = 6 * T * GLA_DK * 4 + 3 * T * GLA_DV * 4
    return pl.pallas_call(
        _gla_kernel,
        grid=(B, H),
        in_specs=[
            blk(Tc, GLA_DK), blk(Tc, GLA_DV), blk(Tc, GLA_DK), blk(Tc, GLA_DK, H),
            blk(T, GLA_DK), blk(T, GLA_DK), blk(T, GLA_DV), blk(T, GLA_DK), blk(T, GLA_DK, H),
            blk(T, GLA_DV),
            pl.BlockSpec((1, GLA_DV), lambda b, h: (0, 0)),
        ],
        out_specs=blk(T, GLA_DV),
        out_shape=jax.ShapeDtypeStruct((B, T, GLA_V_W), BF16),
        scratch_shapes=scratch,
        compiler_params=pltpu.CompilerParams(
            dimension_semantics=("arbitrary", "arbitrary"),
            vmem_limit_bytes=_vmem_limit(io_bytes + scratch_bytes + temp_bytes),
        ),
        name="gla",
    )(gk_c, gv_c, la_c, la_c, gq, gk, gv, la, la, sgg, gn)


def _merge_ffn_kernel(attn_ref, go_ref, smg_ref, x_ref, g1_ref, sh2_ref, sc2_ref, g2_ref,
                      n2_ref, wa_ref, wg_ref, wo_ref, w1_ref, w2_ref, o_ref):
    blocks = [slice(r0, r0 + MERGE_ROWS) for r0 in range(0, x_ref.shape[1], MERGE_ROWS)]
    branch = [(_dot(attn_ref[0, r, :], wa_ref[...]), _dot(go_ref[0, r, :], wg_ref[...])) for r in blocks]
    x1 = []
    for r, (ya, yg) in zip(blocks, branch):
        smg = smg_ref[0, r, :]
        merged = smg[:, :D_MODEL].astype(F32) * ya + smg[:, D_MODEL:].astype(F32) * yg
        x1.append(x_ref[0, r, :] + g1_ref[0] * _dot(merged.astype(BF16), wo_ref[...]))
    hidden = []
    for xr in x1:
        h2 = _modulated_norm(xr, n2_ref[...], sh2_ref[0], sc2_ref[0]).astype(BF16)
        hidden.append((_dot(h2, w1_ref[:, :D_FF]), _dot(h2, w1_ref[:, D_FF:])))
    for r, xr, (a, b) in zip(blocks, x1, hidden):
        u = (a * _sigmoid(a) * b).astype(BF16)
        o_ref[0, r, :] = xr + g2_ref[0] * _dot(u, w2_ref[...])


def _merge_ffn(attn, go, smg, x, g1, sh2, sc2, g2, n2, wa, wg, wo, w1, w2, tm):
    B, T, D = x.shape
    weights = (n2, wa, wg, wo, w1, w2)
    w_bytes = sum(int(np.prod(w.shape)) * w.dtype.itemsize for w in weights)
    tile_bytes = tm * (D * 2 + D * 2 + 2 * D * 2 + D * 4 + D * 4)
    temp_bytes = tm * (2 * D_FF * 4 + D_FF * 2 + 6 * D * 4)
    return pl.pallas_call(
        _merge_ffn_kernel,
        grid=(B, T // tm),
        in_specs=[_row_spec(tm, D), _row_spec(tm, D), _row_spec(tm, 2 * D), _row_spec(tm, D)]
        + [_vec_spec(D)] * 4 + [_const_spec(w.shape) for w in weights],
        out_specs=_row_spec(tm, D),
        out_shape=jax.ShapeDtypeStruct((B, T, D), F32),
        compiler_params=pltpu.CompilerParams(
            dimension_semantics=("arbitrary", "arbitrary"),
            vmem_limit_bytes=_vmem_limit(w_bytes + 2 * tile_bytes + temp_bytes),
        ),
        name="merge_ffn",
    )(attn, go, smg, x, g1, sh2, sc2, g2, n2, wa, wg, wo, w1, w2)


def _deinterleave_heads(w, heads):
    rows = w.shape[0]
    return w.reshape(rows, heads, HEAD_DIM // 2, 2).swapaxes(2, 3).reshape(rows, heads * HEAD_DIM)


def _deinterleave_perm(heads):
    one = np.concatenate([np.arange(0, HEAD_DIM, 2), np.arange(1, HEAD_DIM, 2)])
    return np.concatenate([h * HEAD_DIM + one for h in range(heads)])


def _rope_tables(T):
    rows = T // GRID_W
    row = jnp.repeat(jnp.arange(rows, dtype=F32), GRID_W)
    col = jnp.tile(jnp.arange(GRID_W, dtype=F32), rows)
    half = HEAD_DIM // 2
    inv_freq = 1.0 / (ROPE_THETA ** (jnp.arange(0, half, 2, dtype=F32) / half))
    ang = jnp.concatenate([row[:, None] * inv_freq[None], col[:, None] * inv_freq[None]], axis=-1)
    cos, sin = jnp.cos(ang), jnp.sin(ang)
    return jnp.concatenate([cos, cos], axis=-1), jnp.concatenate([-sin, sin], axis=-1)


def _block(x, c, ctx, c_ctx, w_ada, b_ada, norm1_g, w_in, q_norm_g, k_norm_g, gk_up_f, gk_up_f_b, gk_up_b, gk_up_b_b,
           gla_norm_g, w_attn_proj, w_gla_proj, w_out, norm2_g, w_ffn_in, w_ffn_out):
    B, T, D = x.shape
    Tc = ctx.shape[1]

    mod = _adaln(jnp.concatenate([c, c_ctx[None, :]], axis=0), w_ada, b_ada)
    sh1, sc1, g1, sh2, sc2, g2 = [m[:B, None, :] for m in jnp.split(mod, 6, axis=-1)]
    sh1c, sc1c = [jnp.broadcast_to(m[B:, None, :], (B, 1, D)) for m in jnp.split(mod, 6, axis=-1)[:2]]

    offs = np.concatenate([[0], np.cumsum(IN_WIDTHS)])
    cols = [w_in[:, offs[i]:offs[i + 1]] for i in range(len(IN_WIDTHS))]
    w_q = _deinterleave_heads(cols[0].astype(BF16), N_Q_HEADS)
    w_k = _deinterleave_heads(cols[1].astype(BF16), N_KV_HEADS)
    wqkv = jnp.concatenate([w_q, w_k, cols[2].astype(BF16)], axis=1)
    wgla = jnp.concatenate([c_.astype(BF16) for c_ in cols[3:7]], axis=1)
    wlr = jnp.pad(cols[7], ((0, 0), (0, LR_PAD - 2 * GLA_LOWRANK))).astype(BF16)
    wmg = cols[8].astype(BF16)
    up = jnp.zeros((LR_PAD, 2 * GLA_QK_W), F32)
    up = up.at[:GLA_LOWRANK, :GLA_QK_W].set(gk_up_f).at[GLA_LOWRANK:2 * GLA_LOWRANK, GLA_QK_W:].set(gk_up_b).astype(BF16)
    upb = jnp.concatenate([gk_up_f_b, gk_up_b_b])[None, :]
    perm1 = _deinterleave_perm(1)
    qg = (q_norm_g[perm1] * (HEAD_DIM ** -0.5 * LOG2_E))[None, :]
    kg = k_norm_g[perm1][None, :]
    n1 = norm1_g[None, :]
    cosf, sinf = _rope_tables(T)

    k_c, v_c, gk_c, gv_c, la_c = _inproj_context(ctx, sh1c, sc1c, n1, wqkv, wgla, wlr, up, upb, kg, tm=min(Tc, 256))
    q, k_x, v_x, gq, gk, gv, sgg, la, smg = _inproj_latent(
        x, sh1, sc1, n1, wqkv, wgla, wlr, wmg, up, upb, qg, kg, cosf, sinf, tm=min(T, 512))

    attn = _attention(q, k_c, v_c, k_x, v_x, tq=min(T, 2048))
    go = _gla(gk_c, gv_c, la_c, gq, gk, gv, la, sgg, gla_norm_g[None, :])
    return _merge_ffn(attn, go, smg, x, g1, sh2, sc2, g2, norm2_g[None, :], w_attn_proj.astype(BF16),
                      w_gla_proj.astype(BF16), w_out.astype(BF16), w_ffn_in.astype(BF16), w_ffn_out.astype(BF16),
                      tm=min(T, 512))


def kernel(x, c, ctx, c_ctx, w_ada, b_ada, norm1_g, w_in, q_norm_g, k_norm_g, gk_up_f, gk_up_f_b, gk_up_b, gk_up_b_b,
           gla_norm_g, w_attn_proj, w_gla_proj, w_out, norm2_g, w_ffn_in, w_ffn_out):
    assert w_ada.shape[0] == 1, "single trunk layer"
    return _block(x, c, ctx, c_ctx, w_ada[0], b_ada[0], norm1_g[0], w_in[0], q_norm_g[0], k_norm_g[0], gk_up_f[0],
                  gk_up_f_b[0], gk_up_b[0], gk_up_b_b[0], gla_norm_g[0], w_attn_proj[0], w_gla_proj[0], w_out[0],
                  norm2_g[0], w_ffn_in[0], w_ffn_out[0])
```

```python
import numpy as np
import jax
import jax.numpy as jnp
from jax import lax
from jax.experimental import pallas as pl
from jax.experimental.pallas import tpu as pltpu

F32 = jnp.float32
BF16 = jnp.bfloat16

D_MODEL = 1024
GRID_W = 64
HEAD_DIM = 128
N_Q_HEADS = 8
N_KV_HEADS = 2
Q_PER_KV = N_Q_HEADS // N_KV_HEADS
ROPE_THETA = 10000.0
GLA_HEADS = 4
GLA_DK = 128
GLA_DV = 256
GLA_LOWRANK = 16
GLA_GATE_NORM = 16.0
GLA_CHUNK = 64
D_FF = 2816
EPS = 1e-6
LOG2_E = 1.4426950408889634
MERGE_ROWS = 256
INPROJ_ROWS = 256
SCAN_ROWS = 256
ATTN_ROWS = 512
ATTN_AHEAD = 3

ATTN_Q_W = N_Q_HEADS * HEAD_DIM
ATTN_KV_W = N_KV_HEADS * HEAD_DIM
GLA_QK_W = GLA_HEADS * GLA_DK
GLA_V_W = GLA_HEADS * GLA_DV
IN_WIDTHS = (ATTN_Q_W, ATTN_KV_W, ATTN_KV_W, GLA_QK_W, GLA_QK_W, GLA_V_W, GLA_V_W, 2 * GLA_LOWRANK, 2 * D_MODEL)
LANES = 128
LR_PAD = LANES
V7X_VMEM_CAP = 56 * 1024 * 1024


def _vmem_limit(nbytes):
    return int(min(max(nbytes, 16 * 1024 * 1024), V7X_VMEM_CAP))


def _const_spec(shape):
    nd = len(shape)
    return pl.BlockSpec(shape, lambda *_: (0,) * nd, pipeline_mode=pl.Buffered(1))


def _rms(x):
    return x * lax.rsqrt(jnp.mean(x * x, axis=-1, keepdims=True) + EPS)


def _sigmoid(x):
    return 1.0 / (1.0 + jnp.exp(-x))


def _log_sigmoid(z):
    return jnp.minimum(z, 0.0) - jnp.log(1.0 + jnp.exp(-jnp.abs(z)))


def _dot(a, b):
    return jnp.dot(a, b, preferred_element_type=F32)


def _dot_nt(a, b):
    return lax.dot_general(a, b, (((1,), (1,)), ((), ())), preferred_element_type=F32)


def _dot_tn(a, b):
    return lax.dot_general(a, b, (((0,), (0,)), ((), ())), preferred_element_type=F32)


def _adaln_kernel(c_ref, w_ref, b_ref, o_ref):
    c = c_ref[...]
    s = (c * _sigmoid(c)).astype(BF16)
    o_ref[...] = _dot(s, w_ref[...].astype(BF16)) + b_ref[...]


def _adaln(cvecs, w_ada, b_ada):
    rows, d = cvecs.shape
    n = w_ada.shape[1]
    bn = n // 4
    return pl.pallas_call(
        _adaln_kernel,
        grid=(n // bn,),
        in_specs=[
            pl.BlockSpec((rows, d), lambda j: (0, 0)),
            pl.BlockSpec((d, bn), lambda j: (0, j)),
            pl.BlockSpec((1, bn), lambda j: (0, j)),
        ],
        out_specs=pl.BlockSpec((rows, bn), lambda j: (0, j)),
        out_shape=jax.ShapeDtypeStruct((rows, n), F32),
        compiler_params=pltpu.CompilerParams(
            dimension_semantics=("arbitrary",),
            vmem_limit_bytes=_vmem_limit(3 * d * bn * 4 + 8 * rows * bn * 4),
        ),
        name="adaln",
    )(cvecs, w_ada, b_ada.reshape(1, n))


def _modulated_norm(x, g, shift, scale):
    return (_rms(x) * g) * (1.0 + scale) + shift


def _head_norm(acc, g, heads, rope):
    outs = []
    for h in range(heads):
        y = _rms(acc[:, h * HEAD_DIM:(h + 1) * HEAD_DIM]) * g
        if rope is not None:
            cosf, sinf = rope
            y = y * cosf + pltpu.roll(y, HEAD_DIM // 2, 1) * sinf
        outs.append(y.astype(BF16))
    return outs


def _log_decays(lr, up_ref, upb_ref):
    z = _dot(lr, up_ref[...]) + upb_ref[...]
    return _log_sigmoid(z) * (1.0 / GLA_GATE_NORM)


def _inproj_latent_kernel(x_ref, sh_ref, sc_ref, g_ref, wqkv_ref, wgla_ref, wlr_ref, wmg_ref, up_ref, upb_ref,
                          qg_ref, kg_ref, cos_ref, sin_ref,
                          q_ref, k_ref, v_ref, gq_ref, gk_ref, gv_ref, sgg_ref, la_ref, smg_ref):
    for r0 in range(0, x_ref.shape[1], INPROJ_ROWS):
        rows = slice(r0, r0 + INPROJ_ROWS)
        hb = _modulated_norm(x_ref[0, rows, :], g_ref[...], sh_ref[0], sc_ref[0]).astype(BF16)
        rope = (cos_ref[rows, :], sin_ref[rows, :])
        lr = _dot(hb, wlr_ref[...]).astype(BF16)
        smg_ref[0, rows, :] = _sigmoid(_dot(hb, wmg_ref[...])).astype(BF16)
        la_ref[0, rows, :] = _log_decays(lr, up_ref, upb_ref)
        gg = _dot(hb, wgla_ref[:, 2 * GLA_QK_W + GLA_V_W:])
        sgg_ref[0, rows, :] = (gg * _sigmoid(gg)).astype(BF16)
        q_acc = _dot(hb, wqkv_ref[:, :ATTN_Q_W])
        for h, y in enumerate(_head_norm(q_acc, qg_ref[...], N_Q_HEADS, rope)):
            q_ref[0, rows, h * HEAD_DIM:(h + 1) * HEAD_DIM] = y
        kv_acc = _dot(hb, wqkv_ref[:, ATTN_Q_W:])
        for h, y in enumerate(_head_norm(kv_acc[:, :ATTN_KV_W], kg_ref[...], N_KV_HEADS, rope)):
            k_ref[0, rows, h * HEAD_DIM:(h + 1) * HEAD_DIM] = y
        v_ref[0, rows, :] = kv_acc[:, ATTN_KV_W:].astype(BF16)
        gqk = _dot(hb, wgla_ref[:, :2 * GLA_QK_W])
        gq_ref[0, rows, :] = gqk[:, :GLA_QK_W].astype(BF16)
        gk_ref[0, rows, :] = gqk[:, GLA_QK_W:].astype(BF16)
        gv_ref[0, rows, :] = _dot(hb, wgla_ref[:, 2 * GLA_QK_W:2 * GLA_QK_W + GLA_V_W]).astype(BF16)


def _inproj_context_kernel(x_ref, sh_ref, sc_ref, g_ref, wkv_ref, wgla_ref, wlr_ref, up_ref, upb_ref, kg_ref,
                           k_ref, v_ref, gk_ref, gv_ref, la_ref):
    hb = _modulated_norm(x_ref[0], g_ref[...], sh_ref[0], sc_ref[0]).astype(BF16)
    lr = _dot(hb, wlr_ref[...]).astype(BF16)
    kv_acc = _dot(hb, wkv_ref[:, ATTN_Q_W:])
    la_ref[0] = _log_decays(lr, up_ref, upb_ref)
    for h, y in enumerate(_head_norm(kv_acc[:, :ATTN_KV_W], kg_ref[...], N_KV_HEADS, None)):
        k_ref[0, :, h * HEAD_DIM:(h + 1) * HEAD_DIM] = y
    v_ref[0] = kv_acc[:, ATTN_KV_W:].astype(BF16)
    gk_ref[0] = _dot(hb, wgla_ref[:, GLA_QK_W:2 * GLA_QK_W]).astype(BF16)
    gv_ref[0] = _dot(hb, wgla_ref[:, 2 * GLA_QK_W:2 * GLA_QK_W + GLA_V_W]).astype(BF16)


def _row_spec(tm, width):
    return pl.BlockSpec((1, tm, width), lambda b, j: (b, j, 0))


def _vec_spec(width):
    return pl.BlockSpec((1, 1, width), lambda b, j: (b, 0, 0))


def _inproj_latent(x, sh, sc, g, wqkv, wgla, wlr, wmg, up, upb, qg, kg, cosf, sinf, tm):
    B, T, D = x.shape
    out_widths = (ATTN_Q_W, ATTN_KV_W, ATTN_KV_W, GLA_QK_W, GLA_QK_W, GLA_V_W, GLA_V_W, 2 * GLA_QK_W, 2 * D_MODEL)
    out_dtypes = (BF16,) * 7 + (F32, BF16)
    weights = (g, wqkv, wgla, wlr, wmg, up, upb, qg, kg)
    w_bytes = sum(int(np.prod(w.shape)) * w.dtype.itemsize for w in weights)
    tile_bytes = tm * D * 4 + sum(tm * w * jnp.dtype(dt).itemsize for w, dt in zip(out_widths, out_dtypes))
    temp_bytes = tm * (D * 6 + 2 * D_MODEL * 4 * 3)
    return pl.pallas_call(
        _inproj_latent_kernel,
        grid=(B, T // tm),
        in_specs=[_row_spec(tm, D), _vec_spec(D), _vec_spec(D)]
        + [_const_spec(w.shape) for w in weights]
        + [pl.BlockSpec((tm, HEAD_DIM), lambda b, j: (j, 0))] * 2,
        out_specs=[_row_spec(tm, w) for w in out_widths],
        out_shape=[jax.ShapeDtypeStruct((B, T, w), dt) for w, dt in zip(out_widths, out_dtypes)],
        compiler_params=pltpu.CompilerParams(
            dimension_semantics=("arbitrary", "arbitrary"),
            vmem_limit_bytes=_vmem_limit(w_bytes + 2 * tile_bytes + temp_bytes),
        ),
        name="inproj_latent",
    )(x, sh, sc, g, wqkv, wgla, wlr, wmg, up, upb, qg, kg, cosf, sinf)


def _inproj_context(ctx, sh, sc, g, wkv, wgla, wlr, up, upb, kg, tm):
    B, Tc, D = ctx.shape
    out_widths = (ATTN_KV_W, ATTN_KV_W, GLA_QK_W, GLA_V_W, 2 * GLA_QK_W)
    out_dtypes = (BF16,) * 4 + (F32,)
    weights = (g, wkv, wgla, wlr, up, upb, kg)
    w_bytes = sum(int(np.prod(w.shape)) * w.dtype.itemsize for w in weights)
    tile_bytes = tm * D * 4 + sum(tm * w * jnp.dtype(dt).itemsize for w, dt in zip(out_widths, out_dtypes))
    temp_bytes = tm * (D * 6 + 2 * D_MODEL * 4 * 3)
    return pl.pallas_call(
        _inproj_context_kernel,
        grid=(B, Tc // tm),
        in_specs=[_row_spec(tm, D), _vec_spec(D), _vec_spec(D)] + [_const_spec(w.shape) for w in weights],
        out_specs=[_row_spec(tm, w) for w in out_widths],
        out_shape=[jax.ShapeDtypeStruct((B, Tc, w), dt) for w, dt in zip(out_widths, out_dtypes)],
        compiler_params=pltpu.CompilerParams(
            dimension_semantics=("arbitrary", "arbitrary"),
            vmem_limit_bytes=_vmem_limit(w_bytes + 2 * tile_bytes + temp_bytes),
        ),
        name="inproj_context",
    )(ctx, sh, sc, g, wkv, wgla, wlr, up, upb, kg)


def _attn_kernel(q_ref, kc_ref, vc_ref, kx_ref, vx_ref, o_ref, k_all, v_aug):
    Tc = kc_ref.shape[1]

    @pl.when(pl.program_id(2) == 0)
    def _():
        k_all[:Tc, :] = kc_ref[0]
        k_all[Tc:, :] = kx_ref[0]
        v_aug[:Tc, :HEAD_DIM] = vc_ref[0]
        v_aug[Tc:, :HEAD_DIM] = vx_ref[0]
        v_aug[:, HEAD_DIM:] = jnp.ones((v_aug.shape[0], HEAD_DIM), BF16)

    chains = [(slice(r * ATTN_ROWS, (r + 1) * ATTN_ROWS), slice(g * HEAD_DIM, (g + 1) * HEAD_DIM))
              for r in range(q_ref.shape[1] // ATTN_ROWS) for g in range(Q_PER_KV)]

    def scores(chain):
        rows, lanes = chain
        return _dot_nt(q_ref[0, rows, lanes], k_all[...])

    pending = [scores(c) for c in chains[:ATTN_AHEAD]]
    for i, (rows, lanes) in enumerate(chains):
        s = pending.pop(0)
        if i + ATTN_AHEAD < len(chains):
            pending.append(scores(chains[i + ATTN_AHEAD]))
        p = jnp.exp2(s - jnp.max(s, axis=-1, keepdims=True)).astype(BF16)
        oa = _dot(p, v_aug[...])
        o_ref[0, rows, lanes] = (oa[:, :HEAD_DIM] / oa[:, HEAD_DIM:]).astype(BF16)


def _attention(q, k_c, v_c, k_x, v_x, tq):
    B, T, _ = q.shape
    Tc = k_c.shape[1]
    gw = Q_PER_KV * HEAD_DIM
    score_bytes = Q_PER_KV * tq * (T + Tc) * (4 + 2)
    io_bytes = 2 * (2 * tq * gw * 2 + 2 * (T + Tc) * HEAD_DIM * 2) + (T + Tc) * 3 * HEAD_DIM * 2
    return pl.pallas_call(
        _attn_kernel,
        grid=(B, N_KV_HEADS, T // tq),
        in_specs=[
            pl.BlockSpec((1, tq, gw), lambda b, h, j: (b, j, h)),
            pl.BlockSpec((1, Tc, HEAD_DIM), lambda b, h, j: (b, 0, h)),
            pl.BlockSpec((1, Tc, HEAD_DIM), lambda b, h, j: (b, 0, h)),
            pl.BlockSpec((1, T, HEAD_DIM), lambda b, h, j: (b, 0, h)),
            pl.BlockSpec((1, T, HEAD_DIM), lambda b, h, j: (b, 0, h)),
        ],
        out_specs=pl.BlockSpec((1, tq, gw), lambda b, h, j: (b, j, h)),
        out_shape=jax.ShapeDtypeStruct((B, T, ATTN_Q_W), BF16),
        scratch_shapes=[pltpu.VMEM((Tc + T, HEAD_DIM), BF16), pltpu.VMEM((Tc + T, 2 * HEAD_DIM), BF16)],
        compiler_params=pltpu.CompilerParams(
            dimension_semantics=("arbitrary", "arbitrary", "arbitrary"),
            vmem_limit_bytes=_vmem_limit(score_bytes + io_bytes),
        ),
        name="attention",
    )(q, k_c, v_c, k_x, v_x)


def _chunk_scan(x, reverse):
    n, dk = x.shape
    g = min(n, SCAN_ROWS)
    r = lax.broadcasted_iota(jnp.int32, (g, g), 0)
    c = lax.broadcasted_iota(jnp.int32, (g, g), 1)
    same_chunk = (r // GLA_CHUNK) == (c // GLA_CHUNK)
    tri = jnp.where(same_chunk, jnp.where((c >= r) if reverse else (c <= r), 1.0, 0.0), 0.0).astype(BF16)
    hi = x.astype(BF16)
    lo = (x - hi.astype(F32)).astype(BF16)
    parts = jnp.concatenate([hi, lo], axis=1)
    sums = []
    for i in range(n // g):
        y = _dot(tri, parts[i * g:(i + 1) * g, :])
        sums.append(y[:, :dk] + y[:, dk:])
    return jnp.concatenate(sums, axis=0)


def _gla_prep(q, k, la, reverse, qe_ref, ke_ref, tot_ref, first_row):
    b = _chunk_scan(la, reverse)
    ke_ref[...] = (k.astype(F32) * jnp.exp(-b)).astype(BF16)
    if q is not None:
        qe_ref[...] = (q.astype(F32) * jnp.exp(b) * (GLA_DK ** -0.5)).astype(BF16)
    last = 0 if reverse else GLA_CHUNK - 1
    for n in range(la.shape[0] // GLA_CHUNK):
        row = n * GLA_CHUNK + last
        tot_ref[first_row + n:first_row + n + 1, :] = b[row:row + 1, :]


def _gla_kernel(kc_ref, vc_ref, lafc_ref, labc_ref, q_ref, k_ref, v_ref, laf_ref, lab_ref, sgg_ref, gn_ref,
                o_ref, acc_ref, qe_ref, ke_ref, kec_ref, tot_ref, u_ref):
    C = GLA_CHUNK
    n_ctx = kc_ref.shape[1] // C
    n_lat = q_ref.shape[1] // C

    tot_ref[...] = jnp.zeros_like(tot_ref)
    for d, (lac, la) in enumerate(((lafc_ref, laf_ref), (labc_ref, lab_ref))):
        _gla_prep(q_ref[0], k_ref[0], la[0], d == 1, qe_ref.at[d], ke_ref.at[d], tot_ref.at[d], 0)
        _gla_prep(None, kc_ref[0], lac[0], d == 1, None, kec_ref.at[d], tot_ref.at[d], n_lat)
    decay = [jnp.exp(tot_ref[d].T) for d in range(2)]

    def rows(i):
        return slice(i * C, (i + 1) * C)

    r_id = lax.broadcasted_iota(jnp.int32, (C, C), 0)
    c_id = lax.broadcasted_iota(jnp.int32, (C, C), 1)
    scores = []
    for n in range(n_lat):
        r = rows(n)
        a = (jnp.where(c_id <= r_id, _dot_nt(qe_ref[0, r, :], ke_ref[0, r, :]), 0.0)
             + jnp.where(c_id >= r_id, _dot_nt(qe_ref[1, r, :], ke_ref[1, r, :]), 0.0))
        scores.append(a.astype(BF16))

    for d in range(2):
        for n in range(n_lat):
            u_ref[d, n] = _dot_tn(ke_ref[d, rows(n), :], v_ref[0, rows(n), :])
        for n in range(n_ctx):
            u_ref[d, n_lat + n] = _dot_tn(kec_ref[d, rows(n), :], vc_ref[0, rows(n), :])

    for n in range(n_lat):
        acc_ref[rows(n), :] = _dot(scores[n], v_ref[0, rows(n), :])

    def step(state, d, n):
        col = decay[d][:, n:n + 1]
        return col * state + col * u_ref[d, n]

    s_fwd = jnp.zeros((GLA_DK, GLA_DV), F32)
    s_bwd = jnp.zeros((GLA_DK, GLA_DV), F32)
    for i in range(n_ctx):
        s_fwd = step(s_fwd, 0, n_lat + i)
        s_bwd = step(s_bwd, 1, n_lat + n_ctx - 1 - i)

    for i in range(n_lat):
        j = n_lat - 1 - i
        acc_ref[rows(i), :] += _dot(qe_ref[0, rows(i), :], s_fwd.astype(BF16))
        s_fwd = step(s_fwd, 0, i)
        acc_ref[rows(j), :] += _dot(qe_ref[1, rows(j), :], s_bwd.astype(BF16))
        s_bwd = step(s_bwd, 1, j)

    go = _rms(acc_ref[...]) * gn_ref[...] * sgg_ref[0].astype(F32)
    o_ref[0] = go.astype(BF16)


def _gla(gk_c, gv_c, la_c, gq, gk, gv, la, sgg, gn):
    B, T, _ = gq.shape
    Tc = gk_c.shape[1]
    H = GLA_HEADS
    n_lat, n_ctx = T // GLA_CHUNK, Tc // GLA_CHUNK

    def blk(t, w, off=0):
        return pl.BlockSpec((1, t, w), lambda b, h: (b, 0, h + off))

    io_bytes = 2 * (Tc * (GLA_DK * 2 + GLA_DV * 2 + 2 * GLA_DK * 4)
                    + T * (2 * GLA_DK * 2 + GLA_DV * 2 + 2 * GLA_DK * 4 + 2 * GLA_DV * 2))
    scratch = [
        pltpu.VMEM((T, GLA_DV), F32),
        pltpu.VMEM((2, T, GLA_DK), BF16),
        pltpu.VMEM((2, T, GLA_DK), BF16),
        pltpu.VMEM((2, Tc, GLA_DK), BF16),
        pltpu.VMEM((2, GLA_DK, GLA_DK), F32),
        pltpu.VMEM((2, n_lat + n_ctx, GLA_DK, GLA_DV), F32),
    ]
    assert n_lat + n_ctx <= GLA_DK
    scratch_bytes = (T * GLA_DV * 4 + 4 * (T + Tc) * GLA_DK * 2 + 2 * GLA_DK * GLA_DK * 4
                     + 2 * (n_lat + n_ctx) * GLA_DK * GLA_DV * 4)
    temp_bytes = 6 * T * GLA_DK * 4 + 3 * T * GLA_DV * 4
    return pl.pallas_call(
        _gla_kernel,
        grid=(B, H),
        in_specs=[
            blk(Tc, GLA_DK), blk(Tc, GLA_DV), blk(Tc, GLA_DK), blk(Tc, GLA_DK, H),
            blk(T, GLA_DK), blk(T, GLA_DK), blk(T, GLA_DV), blk(T, GLA_DK), blk(T, GLA_DK, H),
            blk(T, GLA_DV),
            pl.BlockSpec((1, GLA_DV), lambda b, h: (0, 0)),
        ],
        out_specs=blk(T, GLA_DV),
        out_shape=jax.ShapeDtypeStruct((B, T, GLA_V_W), BF16),
        scratch_shapes=scratch,
        compiler_params=pltpu.CompilerParams(
            dimension_semantics=("arbitrary", "arbitrary"),
            vmem_limit_bytes=_vmem_limit(io_bytes + scratch_bytes + temp_bytes),
        ),
        name="gla",
    )(gk_c, gv_c, la_c, la_c, gq, gk, gv, la, la, sgg, gn)


def _merge_ffn_kernel(attn_ref, go_ref, smg_ref, x_ref, g1_ref, sh2_ref, sc2_ref, g2_ref,
                      n2_ref, wa_ref, wg_ref, wo_ref, w1_ref, w2_ref, o_ref):
    blocks = [slice(r0, r0 + MERGE_ROWS) for r0 in range(0, x_ref.shape[1], MERGE_ROWS)]
    branch = [(_dot(attn_ref[0, r, :], wa_ref[...]), _dot(go_ref[0, r, :], wg_ref[...])) for r in blocks]
    x1 = []
    for r, (ya, yg) in zip(blocks, branch):
        smg = smg_ref[0, r, :]
        merged = smg[:, :D_MODEL].astype(F32) * ya + smg[:, D_MODEL:].astype(F32) * yg
        x1.append(x_ref[0, r, :] + g1_ref[0] * _dot(merged.astype(BF16), wo_ref[...]))
    hidden = []
    for xr in x1:
        h2 = _modulated_norm(xr, n2_ref[...], sh2_ref[0], sc2_ref[0]).astype(BF16)
        hidden.append((_dot(h2, w1_ref[:, :D_FF]), _dot(h2, w1_ref[:, D_FF:])))
    for r, xr, (a, b) in zip(blocks, x1, hidden):
        u = (a * _sigmoid(a) * b).astype(BF16)
        o_ref[0, r, :] = xr + g2_ref[0] * _dot(u, w2_ref[...])


def _merge_ffn(attn, go, smg, x, g1, sh2, sc2, g2, n2, wa, wg, wo, w1, w2, tm):
    B, T, D = x.shape
    weights = (n2, wa, wg, wo, w1, w2)
    w_bytes = sum(int(np.prod(w.shape)) * w.dtype.itemsize for w in weights)
    tile_bytes = tm * (D * 2 + D * 2 + 2 * D * 2 + D * 4 + D * 4)
    temp_bytes = tm * (2 * D_FF * 4 + D_FF * 2 + 6 * D * 4)
    return pl.pallas_call(
        _merge_ffn_kernel,
        grid=(B, T // tm),
        in_specs=[_row_spec(tm, D), _row_spec(tm, D), _row_spec(tm, 2 * D), _row_spec(tm, D)]
        + [_vec_spec(D)] * 4 + [_const_spec(w.shape) for w in weights],
        out_specs=_row_spec(tm, D),
        out_shape=jax.ShapeDtypeStruct((B, T, D), F32),
        compiler_params=pltpu.CompilerParams(
            dimension_semantics=("arbitrary", "arbitrary"),
            vmem_limit_bytes=_vmem_limit(w_bytes + 2 * tile_bytes + temp_bytes),
        ),
        name="merge_ffn",
    )(attn, go, smg, x, g1, sh2, sc2, g2, n2, wa, wg, wo, w1, w2)


def _deinterleave_heads(w, heads):
    rows = w.shape[0]
    return w.reshape(rows, heads, HEAD_DIM // 2, 2).swapaxes(2, 3).reshape(rows, heads * HEAD_DIM)


def _deinterleave_perm(heads):
    one = np.concatenate([np.arange(0, HEAD_DIM, 2), np.arange(1, HEAD_DIM, 2)])
    return np.concatenate([h * HEAD_DIM + one for h in range(heads)])


def _rope_tables(T):
    rows = T // GRID_W
    row = jnp.repeat(jnp.arange(rows, dtype=F32), GRID_W)
    col = jnp.tile(jnp.arange(GRID_W, dtype=F32), rows)
    half = HEAD_DIM // 2
    inv_freq = 1.0 / (ROPE_THETA ** (jnp.arange(0, half, 2, dtype=F32) / half))
    ang = jnp.concatenate([row[:, None] * inv_freq[None], col[:, None] * inv_freq[None]], axis=-1)
    cos, sin = jnp.cos(ang), jnp.sin(ang)
    return jnp.concatenate([cos, cos], axis=-1), jnp.concatenate([-sin, sin], axis=-1)


def _block(x, c, ctx, c_ctx, w_ada, b_ada, norm1_g, w_in, q_norm_g, k_norm_g, gk_up_f, gk_up_f_b, gk_up_b, gk_up_b_b,
           gla_norm_g, w_attn_proj, w_gla_proj, w_out, norm2_g, w_ffn_in, w_ffn_out):
    B, T, D = x.shape
    Tc = ctx.shape[1]

    mod = _adaln(jnp.concatenate([c, c_ctx[None, :]], axis=0), w_ada, b_ada)
    sh1, sc1, g1, sh2, sc2, g2 = [m[:B, None, :] for m in jnp.split(mod, 6, axis=-1)]
    sh1c, sc1c = [jnp.broadcast_to(m[B:, None, :], (B, 1, D)) for m in jnp.split(mod, 6, axis=-1)[:2]]

    offs = np.concatenate([[0], np.cumsum(IN_WIDTHS)])
    cols = [w_in[:, offs[i]:offs[i + 1]] for i in range(len(IN_WIDTHS))]
    w_q = _deinterleave_heads(cols[0].astype(BF16), N_Q_HEADS)
    w_k = _deinterleave_heads(cols[1].astype(BF16), N_KV_HEADS)
    wqkv = jnp.concatenate([w_q, w_k, cols[2].astype(BF16)], axis=1)
    wgla = jnp.concatenate([c_.astype(BF16) for c_ in cols[3:7]], axis=1)
    wlr = jnp.pad(cols[7], ((0, 0), (0, LR_PAD - 2 * GLA_LOWRANK))).astype(BF16)
    wmg = cols[8].astype(BF16)
    up = jnp.zeros((LR_PAD, 2 * GLA_QK_W), F32)
    up = up.at[:GLA_LOWRANK, :GLA_QK_W].set(gk_up_f).at[GLA_LOWRANK:2 * GLA_LOWRANK, GLA_QK_W:].set(gk_up_b).astype(BF16)
    upb = jnp.concatenate([gk_up_f_b, gk_up_b_b])[None, :]
    perm1 = _deinterleave_perm(1)
    qg = (q_norm_g[perm1] * (HEAD_DIM ** -0.5 * LOG2_E))[None, :]
    kg = k_norm_g[perm1][None, :]
    n1 = norm1_g[None, :]
    cosf, sinf = _rope_tables(T)

    k_c, v_c, gk_c, gv_c, la_c = _inproj_context(ctx, sh1c, sc1c, n1, wqkv, wgla, wlr, up, upb, kg, tm=min(Tc, 256))
    q, k_x, v_x, gq, gk, gv, sgg, la, smg = _inproj_latent(
        x, sh1, sc1, n1, wqkv, wgla, wlr, wmg, up, upb, qg, kg, cosf, sinf, tm=min(T, 512))

    attn = _attention(q, k_c, v_c, k_x, v_x, tq=min(T, 2048))
    go = _gla(gk_c, gv_c, la_c, gq, gk, gv, la, sgg, gla_norm_g[None, :])
    return _merge_ffn(attn, go, smg, x, g1, sh2, sc2, g2, norm2_g[None, :], w_attn_proj.astype(BF16),
                      w_gla_proj.astype(BF16), w_out.astype(BF16), w_ffn_in.astype(BF16), w_ffn_out.astype(BF16),
                      tm=min(T, 512))


def kernel(x, c, ctx, c_ctx, w_ada, b_ada, norm1_g, w_in, q_norm_g, k_norm_g, gk_up_f, gk_up_f_b, gk_up_b, gk_up_b_b,
           gla_norm_g, w_attn_proj, w_gla_proj, w_out, norm2_g, w_ffn_in, w_ffn_out):
    assert w_ada.shape[0] == 1, "single trunk layer"
    return _block(x, c, ctx, c_ctx, w_ada[0], b_ada[0], norm1_g[0], w_in[0], q_norm_g[0], k_norm_g[0], gk_up_f[0],
                  gk_up_f_b[0], gk_up_b[0], gk_up_b_b[0], gla_norm_g[0], w_attn_proj[0], w_gla_proj[0], w_out[0],
                  norm2_g[0], w_ffn_in[0], w_ffn_out[0])
```
